```python
import math
import jax
import jax.numpy as jnp
from jax import lax
import numpy as np

D_MODEL = 2048
BATCH = 2
SEQ = 4096
DEPTH = 4
DEC_BATCH = 8
DEC_SEQ = 4
PAST_LEN = 16384
PAGE_SIZE = 128

N_MIXERS = 2
N_CONV_LAYERS = (DEPTH + 1) // 2
N_ATTN_LAYERS = DEPTH // 2
CONV_WIDTH = 31
CONV_INNER = D_MODEL
N_HEADS = 8
HEAD_DIM = D_MODEL // (2 * N_HEADS)
V_DIM = 2 * HEAD_DIM
ROPE_THETA = 10000.0
Q_BLOCK = 128
N_GROUPS = 4
EXPERTS_PER_GROUP = 8
N_EXPERTS = N_GROUPS * EXPERTS_PER_GROUP
TOP_K = 2
D_EXPERT = D_MODEL // 4
ALPHA = (2 * DEPTH) ** 0.25
BETA = (8 * DEPTH) ** -0.25
LN_EPS = 1e-5

kernel_name = 'hybrid_conformer_diffattn_hmoe_step'


def layer_norm(x, g, b):
    xf = x.astype(jnp.float32)
    mu = jnp.mean(xf, axis=-1, keepdims=True)
    var = jnp.mean(jnp.square(xf - mu), axis=-1, keepdims=True)
    y = (xf - mu) * lax.rsqrt(var + LN_EPS)
    return (y * g.astype(jnp.float32) + b.astype(jnp.float32)).astype(x.dtype)


def rms_norm(x, g):
    xf = x.astype(jnp.float32)
    y = xf * lax.rsqrt(jnp.mean(jnp.square(xf), axis=-1, keepdims=True) + LN_EPS)
    return (y * g.astype(jnp.float32)).astype(x.dtype)


def ada_mod(c, w, b):
    m = c @ w + b
    return jnp.split(m, 3, axis=-1)


def modulate(x, shift, scale):
    return x * (1.0 + scale[:, None, :]) + shift[:, None, :]


def post_norm(x, out, gate, g, b):
    return layer_norm(ALPHA * x + gate[:, None, :] * out, g, b)


def rope(x, pos):
    half = HEAD_DIM // 2
    inv = ROPE_THETA ** (-jnp.arange(half, dtype=jnp.float32) / half)
    ang = pos.astype(jnp.float32)[:, None] * inv[None, :]
    cos = jnp.cos(ang)[None, :, None, None, :]
    sin = jnp.sin(ang)[None, :, None, None, :]
    xf = x.astype(jnp.float32)
    x1, x2 = xf[..., :half], xf[..., half:]
    out = jnp.concatenate([x1 * cos - x2 * sin, x2 * cos + x1 * sin], axis=-1)
    return out.astype(x.dtype)


def conv_module(h, buf, w_in, b_in, dw, dw_b, ln_g, ln_b, w_out, b_out):
    u = h @ w_in + b_in
    a, g = jnp.split(u, 2, axis=-1)
    glu = a * jax.nn.sigmoid(g)
    xp = jnp.concatenate([buf.astype(glu.dtype), glu], axis=1)
    y = lax.conv_general_dilated(
        xp, dw[:, None, :], window_strides=(1,), padding='VALID',
        dimension_numbers=('NWC', 'WIO', 'NWC'), feature_group_count=CONV_INNER) + dw_b
    y = jax.nn.silu(layer_norm(y, ln_g, ln_b))
    return y @ w_out + b_out, xp[:, -(CONV_WIDTH - 1):]


def diff_attn_core(q, k, v, q_pos, k_pos, lam):
    s = jnp.einsum('bqhjd,bkhjd->bhjqk', q, k).astype(jnp.float32) * (HEAD_DIM ** -0.5)
    mask = k_pos[None, :] <= q_pos[:, None]
    s = jnp.where(mask[None, None, None], s, -jnp.inf)
    p = jax.nn.softmax(s, axis=-1)
    a = p[:, :, 0] - lam * p[:, :, 1]
    return jnp.einsum('bhqk,bkhe->bqhe', a.astype(v.dtype), v)


def diff_attention(h, pos, past_k, past_v, past_pos, w_qkv, lam_vecs, subln_g, w_out, lambda_init):
    bsz, t, _ = h.shape
    qkv = h @ w_qkv
    q, k, v = jnp.split(qkv, 3, axis=-1)
    q = rope(q.reshape(bsz, t, N_HEADS, 2, HEAD_DIM), pos)
    k = rope(k.reshape(bsz, t, N_HEADS, 2, HEAD_DIM), pos)
    v = v.reshape(bsz, t, N_HEADS, V_DIM)
    lv = lam_vecs.astype(jnp.float32)
    lam = jnp.exp(jnp.sum(lv[0] * lv[1])) - jnp.exp(jnp.sum(lv[2] * lv[3])) + lambda_init
    if past_k is None:
        def block(n):
            start = n * Q_BLOCK
            qb = lax.dynamic_slice_in_dim(q, start, Q_BLOCK, axis=1)
            pb = lax.dynamic_slice_in_dim(pos, start, Q_BLOCK, axis=0)
            return diff_attn_core(qb, k, v, pb, pos, lam)
        o = lax.map(block, jnp.arange(t // Q_BLOCK))
        o = jnp.moveaxis(o, 0, 1).reshape(bsz, t, N_HEADS, V_DIM)
    else:
        keys = jnp.concatenate([past_k.astype(k.dtype), k], axis=1)
        vals = jnp.concatenate([past_v.astype(v.dtype), v], axis=1)
        kpos = jnp.concatenate([past_pos, pos], axis=0)
        o = diff_attn_core(q, keys, vals, pos, kpos, lam)
    o = rms_norm(o, subln_g) * (1.0 - lambda_init)
    return o.reshape(bsz, t, N_HEADS * V_DIM) @ w_out, k, v


def hier_moe(h, w_group, b_group, w_expert, b_expert, w_in, w_out):
    shp = h.shape
    t = h.reshape(-1, shp[-1])
    n = t.shape[0]
    glog = (t @ w_group + b_group).astype(jnp.float32)
    pg = jax.nn.softmax(glog, axis=-1)
    gsel = jnp.argmax(glog, axis=-1)
    pg_sel = jnp.take_along_axis(pg, gsel[:, None], axis=1)[:, 0]
    elog = (t @ w_expert + b_expert).astype(jnp.float32).reshape(n, N_GROUPS, EXPERTS_PER_GROUP)
    sel = jnp.take_along_axis(elog, gsel[:, None, None], axis=1)[:, 0]
    top_v, top_i = lax.top_k(sel, TOP_K)
    w = jax.nn.softmax(top_v, axis=-1) * pg_sel[:, None]
    in_group = jnp.sum(jax.nn.one_hot(top_i, EXPERTS_PER_GROUP, dtype=jnp.float32) * w[..., None], axis=1)
    gates = (jax.nn.one_hot(gsel, N_GROUPS, dtype=jnp.float32)[:, :, None] * in_group[:, None, :])
    gates = gates.reshape(n, N_EXPERTS).astype(t.dtype)
    u = jnp.einsum('td,edf->tef', t, w_in)
    a, b = jnp.split(u, 2, axis=-1)
    act = jax.nn.silu(a) * b * gates[:, :, None]
    out = jnp.einsum('tef,efd->td', act, w_out)
    return out.reshape(shp)


def setup_inputs(seed: int = 0) -> dict:
    key = jax.random.key(seed)
    ks = iter(jax.random.split(key, 40))
    f32 = jnp.float32

    def nrm(shape, scale):
        return jax.random.normal(next(ks), shape, f32) * scale

    n_pages = PAST_LEN // PAGE_SIZE
    n_used = DEC_BATCH * n_pages
    n_pool = n_used + n_used // 4
    D = D_MODEL
    inp = {}
    inp['x_prompt'] = nrm((BATCH, SEQ, D), 1.0)
    inp['x_sample'] = nrm((DEC_BATCH, DEC_SEQ, D), 1.0)
    inp['cache_k'] = nrm((N_ATTN_LAYERS, n_pool, PAGE_SIZE, N_HEADS, 2, HEAD_DIM), 1.0)
    inp['cache_v'] = nrm((N_ATTN_LAYERS, n_pool, PAGE_SIZE, N_HEADS, V_DIM), 1.0)
    inp['state_conv'] = nrm((N_CONV_LAYERS, DEC_BATCH, CONV_WIDTH - 1, CONV_INNER), 1.0)
    perm = jax.random.permutation(next(ks), n_pool)[:n_used]
    inp['page_table'] = perm.reshape(DEC_BATCH, n_pages).astype(jnp.int32)
    inp['c_prompt'] = nrm((BATCH, D), 1.0)
    inp['c_sample'] = nrm((DEC_BATCH, D), 1.0)
    inp['ada_w'] = nrm((DEPTH, 2, D, 3 * D), 0.5 * D ** -0.5)
    inp['ada_b'] = nrm((DEPTH, 2, 3 * D), 0.01)
    inp['ln_g'] = 1.0 + nrm((DEPTH, 2, D), 0.01)
    inp['ln_b'] = nrm((DEPTH, 2, D), 0.01)
    inp['conv_w_in'] = nrm((N_CONV_LAYERS, D, 2 * CONV_INNER), D ** -0.5)
    inp['conv_b_in'] = nrm((N_CONV_LAYERS, 2 * CONV_INNER), 0.01)
    inp['conv_dw'] = nrm((N_CONV_LAYERS, CONV_WIDTH, CONV_INNER), CONV_WIDTH ** -0.5)
    inp['conv_dw_b'] = nrm((N_CONV_LAYERS, CONV_INNER), 0.01)
    inp['conv_ln_g'] = 1.0 + nrm((N_CONV_LAYERS, CONV_INNER), 0.01)
    inp['conv_ln_b'] = nrm((N_CONV_LAYERS, CONV_INNER), 0.01)
    inp['conv_w_out'] = nrm((N_CONV_LAYERS, CONV_INNER, D), BETA * CONV_INNER ** -0.5)
    inp['conv_b_out'] = nrm((N_CONV_LAYERS, D), 0.01)
    inp['attn_w_qkv'] = nrm((N_ATTN_LAYERS, D, 3 * D), D ** -0.5)
    inp['attn_lambda'] = nrm((N_ATTN_LAYERS, 4, HEAD_DIM), 0.1)
    inp['attn_subln_g'] = 1.0 + nrm((N_ATTN_LAYERS, V_DIM), 0.01)
    inp['attn_w_out'] = nrm((N_ATTN_LAYERS, N_HEADS * V_DIM, D), BETA * (N_HEADS * V_DIM) ** -0.5)
    inp['moe_w_group'] = nrm((DEPTH, D, N_GROUPS), D ** -0.5)
    inp['moe_b_group'] = nrm((DEPTH, N_GROUPS), 0.01)
    inp['moe_w_expert'] = nrm((DEPTH, D, N_EXPERTS), D ** -0.5)
    inp['moe_b_expert'] = nrm((DEPTH, N_EXPERTS), 0.01)
    inp['moe_w_in'] = nrm((DEPTH, N_EXPERTS, D, 2 * D_EXPERT), D ** -0.5)
    inp['moe_w_out'] = nrm((DEPTH, N_EXPERTS, D_EXPERT, D), BETA * D_EXPERT ** -0.5)
    return inp


def reference(x_prompt, x_sample, cache_k, cache_v, state_conv, page_table, c_prompt, c_sample,
              ada_w, ada_b, ln_g, ln_b,
              conv_w_in, conv_b_in, conv_dw, conv_dw_b, conv_ln_g, conv_ln_b, conv_w_out, conv_b_out,
              attn_w_qkv, attn_lambda, attn_subln_g, attn_w_out,
              moe_w_group, moe_b_group, moe_w_expert, moe_b_expert, moe_w_in, moe_w_out):
    bp, tp, _ = x_prompt.shape
    bs, ts, _ = x_sample.shape
    past_len = page_table.shape[1] * PAGE_SIZE
    pos_p = jnp.arange(tp, dtype=jnp.int32)
    pos_s = past_len + jnp.arange(ts, dtype=jnp.int32)
    past_pos = jnp.arange(past_len, dtype=jnp.int32)
    x_p, x_s = x_prompt, x_sample
    k_p_rows, v_p_rows, conv_p_rows = [], [], []
    k_s_rows, v_s_rows, conv_s_rows = [], [], []
    for i in range(DEPTH):
        l = i // N_MIXERS
        sh_p, sc_p, gt_p = ada_mod(c_prompt, ada_w[i, 0], ada_b[i, 0])
        sh_s, sc_s, gt_s = ada_mod(c_sample, ada_w[i, 0], ada_b[i, 0])
        h_p = modulate(x_p, sh_p, sc_p)
        h_s = modulate(x_s, sh_s, sc_s)
        if i % N_MIXERS == 0:
            cw = (conv_w_in[l], conv_b_in[l], conv_dw[l], conv_dw_b[l],
                  conv_ln_g[l], conv_ln_b[l], conv_w_out[l], conv_b_out[l])
            zero_buf = jnp.zeros((bp, CONV_WIDTH - 1, CONV_INNER), h_p.dtype)
            o_p, buf_p = conv_module(h_p, zero_buf, *cw)
            o_s, buf_s = conv_module(h_s, state_conv[l], *cw)
            conv_p_rows.append(buf_p)
            conv_s_rows.append(buf_s)
        else:
            lam_init = 0.8 - 0.6 * math.exp(-0.3 * i)
            aw = (attn_w_qkv[l], attn_lambda[l], attn_subln_g[l], attn_w_out[l], lam_init)
            o_p, k_p, v_p = diff_attention(h_p, pos_p, None, None, None, *aw)
            past_k = cache_k[l, page_table].reshape(bs, past_len, N_HEADS, 2, HEAD_DIM)
            past_v = cache_v[l, page_table].reshape(bs, past_len, N_HEADS, V_DIM)
            o_s, k_s, v_s = diff_attention(h_s, pos_s, past_k, past_v, past_pos, *aw)
            k_p_rows.append(k_p)
            v_p_rows.append(v_p)
            k_s_rows.append(k_s)
            v_s_rows.append(v_s)
        x_p = post_norm(x_p, o_p, gt_p, ln_g[i, 0], ln_b[i, 0])
        x_s = post_norm(x_s, o_s, gt_s, ln_g[i, 0], ln_b[i, 0])
        sh_p, sc_p, gt_p = ada_mod(c_prompt, ada_w[i, 1], ada_b[i, 1])
        sh_s, sc_s, gt_s = ada_mod(c_sample, ada_w[i, 1], ada_b[i, 1])
        mw = (moe_w_group[i], moe_b_group[i], moe_w_expert[i], moe_b_expert[i], moe_w_in[i], moe_w_out[i])
        o_p = hier_moe(modulate(x_p, sh_p, sc_p), *mw)
        o_s = hier_moe(modulate(x_s, sh_s, sc_s), *mw)
        x_p = post_norm(x_p, o_p, gt_p, ln_g[i, 1], ln_b[i, 1])
        x_s = post_norm(x_s, o_s, gt_s, ln_g[i, 1], ln_b[i, 1])
    return (x_p, x_s, jnp.stack(k_p_rows), jnp.stack(v_p_rows), jnp.stack(conv_p_rows),
            jnp.stack(k_s_rows), jnp.stack(v_s_rows), jnp.stack(conv_s_rows))
```

```python
import functools
import math

import jax
import jax.numpy as jnp
from jax import lax
from jax.experimental import pallas as pl
from jax.experimental.pallas import tpu as pltpu

D_MODEL = 2048
DEPTH = 4
PAGE_SIZE = 128
N_MIXERS = 2
CONV_WIDTH = 31
N_HEADS = 8
HEAD_DIM = D_MODEL // (2 * N_HEADS)
V_DIM = 2 * HEAD_DIM
ROPE_THETA = 10000.0
N_GROUPS = 4
EXPERTS_PER_GROUP = 8
N_EXPERTS = N_GROUPS * EXPERTS_PER_GROUP
TOP_K = 2
D_EXPERT = D_MODEL // 4
ALPHA = (2 * DEPTH) ** 0.25
LN_EPS = 1e-5

F32 = jnp.float32
BF16 = jnp.bfloat16
LANES = 128
HALO = 32
VMEM_LIMIT = 56 * 1024 * 1024
MOE_TM = 256
ROUTE_LANES = 128
NEG_INF = float("-inf")


def _params(*sem):
    return pltpu.CompilerParams(dimension_semantics=sem, vmem_limit_bytes=VMEM_LIMIT)


def _dot(a, b):
    return jnp.dot(a, b, preferred_element_type=F32)


def _dot_nt(a, b):
    return lax.dot_general(a, b, (((1,), (1,)), ((), ())), preferred_element_type=F32)


def _layer_norm(z, g, b):
    mu = jnp.mean(z, axis=-1, keepdims=True)
    zc = z - mu
    var = jnp.mean(zc * zc, axis=-1, keepdims=True)
    return zc * lax.rsqrt(var + LN_EPS) * g + b


class Seg:
    def __init__(self, rows, tm, rows_per_batch, per_row):
        self.rows = rows
        self.tm = tm
        self.nt = rows // tm
        self.per_row = per_row
        self.tiles_per_batch = max(rows_per_batch // tm, 1)

    def mod_block(self):
        return (1, self.tm if self.per_row else 1, D_MODEL)

    def mod_index(self, i):
        return (0 if self.per_row else i // self.tiles_per_batch, 0, 0)

    def pos_index(self, i):
        return (0 if self.per_row else i % self.tiles_per_batch, 0)


def _ada_kernel(c_ref, w_ref, b_ref, o_ref):
    o_ref[0] = _dot(c_ref[...].astype(BF16), w_ref[0].astype(BF16)) + b_ref[0]


def ada_all(c_pad, w, b):
    n_sub, _, n_out = w.shape
    rows = c_pad.shape[0]
    tn = 1024
    return pl.pallas_call(
        _ada_kernel,
        out_shape=jax.ShapeDtypeStruct((n_sub, rows, n_out), F32),
        grid=(n_sub, n_out // tn),
        in_specs=[
            pl.BlockSpec((rows, D_MODEL), lambda s, j: (0, 0)),
            pl.BlockSpec((1, D_MODEL, tn), lambda s, j: (s, 0, j)),
            pl.BlockSpec((1, 1, tn), lambda s, j: (s, 0, j)),
        ],
        out_specs=pl.BlockSpec((1, rows, tn), lambda s, j: (s, 0, j)),
        compiler_params=_params("arbitrary", "arbitrary"),
        name="ada_all",
    )(c_pad, w, b)


def _glu_kernel(x_ref, sh_ref, sc_ref, wa_ref, wg_ref, ba_ref, bg_ref, o_ref, h_scr):
    @pl.when(pl.program_id(1) == 0)
    def _():
        h_scr[...] = (x_ref[...] * (1.0 + sc_ref[0]) + sh_ref[0]).astype(BF16)

    h = h_scr[...]
    a = _dot(h, wa_ref[...]) + ba_ref[...]
    g = _dot(h, wg_ref[...]) + bg_ref[...]
    o_ref[...] = a * jax.nn.sigmoid(g)


def mod_mm_glu(seg, x, shift, scale, w_in, b_in):
    tn = 512
    nj = D_MODEL // tn
    return pl.pallas_call(
        _glu_kernel,
        out_shape=jax.ShapeDtypeStruct((seg.rows, D_MODEL), F32),
        grid=(seg.nt, nj),
        in_specs=[
            pl.BlockSpec((seg.tm, D_MODEL), lambda i, j: (i, 0)),
            pl.BlockSpec(seg.mod_block(), lambda i, j: seg.mod_index(i)),
            pl.BlockSpec(seg.mod_block(), lambda i, j: seg.mod_index(i)),
            pl.BlockSpec((D_MODEL, tn), lambda i, j: (0, j)),
            pl.BlockSpec((D_MODEL, tn), lambda i, j: (0, j + nj)),
            pl.BlockSpec((1, tn), lambda i, j: (0, j)),
            pl.BlockSpec((1, tn), lambda i, j: (0, j + nj)),
        ],
        out_specs=pl.BlockSpec((seg.tm, tn), lambda i, j: (i, j)),
        scratch_shapes=[pltpu.VMEM((seg.tm, D_MODEL), BF16)],
        compiler_params=_params("arbitrary", "arbitrary"),
        name="mod_mm_glu",
    )(x, shift, scale, w_in, w_in, b_in, b_in)


def _dwconv_kernel(cur_ref, halo_ref, dw_ref, dwb_ref, g_ref, b_ref, o_ref, xs_scr, y_scr, *, tt, zero_first):
    hist = halo_ref[0]
    if zero_first:
        hist = jnp.where(pl.program_id(1) == 0, jnp.zeros_like(hist), hist)
    xs_scr[0:HALO, :] = hist
    xs_scr[HALO:HALO + tt, :] = cur_ref[0]

    rch = min(tt, 32)
    cch = 256
    lead = HALO - (CONV_WIDTH - 1)

    def col_body(cc, carry):
        cols = pl.ds(pl.multiple_of(cc * cch, cch), cch)
        for rc in range(tt // rch):
            acc = jnp.zeros((rch, cch), F32)
            for j in range(CONV_WIDTH):
                r0 = rc * rch + lead + j
                acc = acc + dw_ref[j:j + 1, cols] * xs_scr[r0:r0 + rch, cols]
            y_scr[rc * rch:(rc + 1) * rch, cols] = acc + dwb_ref[:, cols]
        return carry

    lax.fori_loop(0, D_MODEL // cch, col_body, 0)
    y = _layer_norm(y_scr[...], g_ref[...], b_ref[...])
    o_ref[0] = (y * jax.nn.sigmoid(y)).astype(BF16)


def dwconv_ln_silu(cur, halo, halo_index, zero_first, tt, dw, dw_b, ln_g, ln_b):
    nb, t, _ = cur.shape
    kern = functools.partial(_dwconv_kernel, tt=tt, zero_first=zero_first)
    return pl.pallas_call(
        kern,
        out_shape=jax.ShapeDtypeStruct((nb, t, D_MODEL), BF16),
        grid=(nb, t // tt),
        in_specs=[
            pl.BlockSpec((1, tt, D_MODEL), lambda b, i: (b, i, 0)),
            pl.BlockSpec((1, HALO, D_MODEL), halo_index),
            pl.BlockSpec((HALO, D_MODEL), lambda b, i: (0, 0)),
            pl.BlockSpec((1, D_MODEL), lambda b, i: (0, 0)),
            pl.BlockSpec((1, D_MODEL), lambda b, i: (0, 0)),
            pl.BlockSpec((1, D_MODEL), lambda b, i: (0, 0)),
        ],
        out_specs=pl.BlockSpec((1, tt, D_MODEL), lambda b, i: (b, i, 0)),
        scratch_shapes=[pltpu.VMEM((HALO + tt, D_MODEL), F32), pltpu.VMEM((tt, D_MODEL), F32)],
        compiler_params=_params("arbitrary", "arbitrary"),
        name="dwconv_ln_silu",
    )(cur, halo, dw, dw_b, ln_g, ln_b)


def _mm_postnorm_kernel(y_ref, w_ref, bias_ref, x_ref, gate_ref, g_ref, b_ref, o_ref):
    out = _dot(y_ref[...], w_ref[...]) + bias_ref[...]
    z = ALPHA * x_ref[...] + gate_ref[0] * out
    o_ref[...] = _layer_norm(z, g_ref[...], b_ref[...])


def mm_postnorm(seg, y, w, bias, x, gate, ln_g, ln_b):
    k = y.shape[1]
    row = lambda i: (0, 0)
    return pl.pallas_call(
        _mm_postnorm_kernel,
        out_shape=jax.ShapeDtypeStruct((seg.rows, D_MODEL), F32),
        grid=(seg.nt,),
        in_specs=[
            pl.BlockSpec((seg.tm, k), lambda i: (i, 0)),
            pl.BlockSpec((k, D_MODEL), row),
            pl.BlockSpec((1, D_MODEL), row),
            pl.BlockSpec((seg.tm, D_MODEL), lambda i: (i, 0)),
            pl.BlockSpec(seg.mod_block(), seg.mod_index),
            pl.BlockSpec((1, D_MODEL), row),
            pl.BlockSpec((1, D_MODEL), row),
        ],
        out_specs=pl.BlockSpec((seg.tm, D_MODEL), lambda i: (i, 0)),
        compiler_params=_params("arbitrary"),
        name="mm_postnorm",
    )(y, w, bias, x, gate, ln_g, ln_b)


def _qkv_kernel(x_ref, sh_ref, sc_ref, w_ref, cos_ref, sin_ref, of_ref, ob_ref, h_scr, *, tn):
    part = pl.program_id(1)

    @pl.when((part == 0) & (pl.program_id(2) == 0))
    def _():
        h_scr[...] = (x_ref[...] * (1.0 + sc_ref[0]) + sh_ref[0]).astype(BF16)

    acc = _dot(h_scr[...], w_ref[...])

    @pl.when(part < 2)
    def _():
        cos = cos_ref[...]
        sin = sin_ref[...]
        qscale = jnp.where(part == 0, HEAD_DIM ** -0.5, 1.0).astype(F32)
        for g in range(tn // HEAD_DIM):
            blk = acc[:, g * HEAD_DIM:(g + 1) * HEAD_DIM]
            rot = blk * cos + pltpu.roll(blk, HEAD_DIM // 2, 1) * sin
            of_ref[0, :, g * HEAD_DIM:(g + 1) * HEAD_DIM] = rot
            ob_ref[0, :, g * HEAD_DIM:(g + 1) * HEAD_DIM] = (rot * qscale).astype(BF16)

    @pl.when(part == 2)
    def _():
        of_ref[0] = acc
        ob_ref[0] = acc.astype(BF16)


def mod_mm_qkv(seg, x, shift, scale, w_qkv, cos, sin):
    tn = 512
    nj = D_MODEL // tn
    kern = functools.partial(_qkv_kernel, tn=tn)
    out_spec = pl.BlockSpec((1, seg.tm, tn), lambda i, p, j: (p, i, j))
    return pl.pallas_call(
        kern,
        out_shape=(jax.ShapeDtypeStruct((3, seg.rows, D_MODEL), F32),
                   jax.ShapeDtypeStruct((3, seg.rows, D_MODEL), BF16)),
        grid=(seg.nt, 3, nj),
        in_specs=[
            pl.BlockSpec((seg.tm, D_MODEL), lambda i, p, j: (i, 0)),
            pl.BlockSpec(seg.mod_block(), lambda i, p, j: seg.mod_index(i)),
            pl.BlockSpec(seg.mod_block(), lambda i, p, j: seg.mod_index(i)),
            pl.BlockSpec((D_MODEL, tn), lambda i, p, j: (0, p * nj + j)),
            pl.BlockSpec((seg.tm, HEAD_DIM), lambda i, p, j: seg.pos_index(i)),
            pl.BlockSpec((seg.tm, HEAD_DIM), lambda i, p, j: seg.pos_index(i)),
        ],
        out_specs=(out_spec, out_spec),
        scratch_shapes=[pltpu.VMEM((seg.tm, D_MODEL), BF16)],
        compiler_params=_params("arbitrary", "arbitrary", "arbitrary"),
        name="mod_mm_qkv",
    )(x, shift, scale, w_qkv, cos, sin)


def _diff_lambda(lam_ref, lam_init):
    lv = lam_ref[...]
    s01 = jnp.sum(lv[0:1] * lv[1:2], axis=-1, keepdims=True)
    s23 = jnp.sum(lv[2:3] * lv[3:4], axis=-1, keepdims=True)
    return jnp.exp(s01) - jnp.exp(s23) + lam_init


def _head_rms(o, g, lam_init):
    ms = jnp.mean(o * o, axis=-1, keepdims=True)
    return o * lax.rsqrt(ms + LN_EPS) * g * (1.0 - lam_init)


def _flash_kernel(qi_ref, ki_ref, q_ref, k_ref, v_ref, lam_ref, g_ref, o_ref, m_scr, l_scr, acc_scr, *, tq, lam_init):
    pr = pl.program_id(2)
    qi = qi_ref[pr]
    ki = ki_ref[pr]

    @pl.when(ki == 0)
    def _():
        m_scr[...] = jnp.full_like(m_scr, NEG_INF)
        l_scr[...] = jnp.zeros_like(l_scr)
        acc_scr[...] = jnp.zeros_like(acc_scr)

    def step(masked):
        q = q_ref[0]
        k = k_ref[0]
        v = v_ref[0]
        if masked:
            row = lax.broadcasted_iota(jnp.int32, (tq, tq), 0)
            col = lax.broadcasted_iota(jnp.int32, (tq, tq), 1)
            keep = col <= row
        for j in range(2):
            s = _dot_nt(q[:, j * HEAD_DIM:(j + 1) * HEAD_DIM], k[:, j * HEAD_DIM:(j + 1) * HEAD_DIM])
            if masked:
                s = jnp.where(keep, s, NEG_INF)
            m_old = m_scr[j]
            m_new = jnp.maximum(m_old, jnp.max(s, axis=1, keepdims=True))
            alpha = jnp.exp(m_old - m_new)
            p = jnp.exp(s - m_new)
            l_scr[j] = alpha * l_scr[j] + jnp.sum(p, axis=1, keepdims=True)
            acc_scr[j] = alpha * acc_scr[j] + _dot(p.astype(BF16), v)
            m_scr[j] = m_new

    @pl.when(ki < qi)
    def _():
        step(False)

    @pl.when(ki == qi)
    def _():
        step(True)
        lam = _diff_lambda(lam_ref, lam_init)
        o = acc_scr[0] / l_scr[0] - lam * (acc_scr[1] / l_scr[1])
        o_ref[...] = _head_rms(o, g_ref[...], lam_init).astype(BF16)


def flash_diff_attention(qkv_bf, nb, t, lam_vecs, subln_g, lam_init):
    tq = 512
    nq = t // tq
    pairs = [(a, b) for a in range(nq) for b in range(a + 1)]
    qi_tab = jnp.asarray([p[0] for p in pairs], jnp.int32)
    ki_tab = jnp.asarray([p[1] for p in pairs], jnp.int32)
    kern = functools.partial(_flash_kernel, tq=tq, lam_init=lam_init)
    grid_spec = pltpu.PrefetchScalarGridSpec(
        num_scalar_prefetch=2,
        grid=(nb, N_HEADS, len(pairs)),
        in_specs=[
            pl.BlockSpec((1, tq, V_DIM), lambda b, h, p, qt, kt: (0, b * nq + qt[p], h)),
            pl.BlockSpec((1, tq, V_DIM), lambda b, h, p, qt, kt: (1, b * nq + kt[p], h)),
            pl.BlockSpec((1, tq, V_DIM), lambda b, h, p, qt, kt: (2, b * nq + kt[p], h)),
            pl.BlockSpec((4, HEAD_DIM), lambda b, h, p, qt, kt: (0, 0)),
            pl.BlockSpec((1, V_DIM), lambda b, h, p, qt, kt: (0, 0)),
        ],
        out_specs=pl.BlockSpec((tq, V_DIM), lambda b, h, p, qt, kt: (b * nq + qt[p], h)),
        scratch_shapes=[
            pltpu.VMEM((2, tq, 1), F32),
            pltpu.VMEM((2, tq, 1), F32),
            pltpu.VMEM((2, tq, V_DIM), F32),
        ],
    )
    return pl.pallas_call(
        kern,
        out_shape=jax.ShapeDtypeStruct((nb * t, D_MODEL), BF16),
        grid_spec=grid_spec,
        compiler_params=_params("arbitrary", "arbitrary", "arbitrary"),
        name="flash_diff_attention",
    )(qi_tab, ki_tab, qkv_bf, qkv_bf, qkv_bf, lam_vecs, subln_g)


def _decode_kernel(pt_ref, qbd_ref, k_ref, v_ref, kn_ref, vn_ref, lam_ref, g_ref, o_ref, m_scr, l_scr, acc_scr,
                   *, n_pages, n_tok, lam_init):
    p = pl.program_id(1)
    n_rows = qbd_ref.shape[1]
    half = n_rows // 2

    @pl.when(p == 0)
    def _():
        m_scr[...] = jnp.full_like(m_scr, NEG_INF)
        l_scr[...] = jnp.zeros_like(l_scr)
        acc_scr[...] = jnp.zeros_like(acc_scr)

    def process(kb, vb, keep):
        s = _dot_nt(qbd_ref[0], kb)
        if keep is not None:
            s = jnp.where(keep, s, NEG_INF)
        m_old = m_scr[...]
        m_new = jnp.maximum(m_old, jnp.max(s, axis=1, keepdims=True))
        alpha = jnp.exp(m_old - m_new)
        pe = jnp.exp(s - m_new)
        l_scr[...] = alpha * l_scr[...] + jnp.sum(pe, axis=1, keepdims=True)
        acc_scr[...] = alpha * acc_scr[...] + _dot(pe.astype(BF16), vb)
        m_scr[...] = m_new

    process(k_ref[0, 0].astype(BF16), v_ref[0, 0].astype(BF16), None)

    @pl.when(p == n_pages - 1)
    def _():
        row = lax.broadcasted_iota(jnp.int32, (n_rows, PAGE_SIZE), 0)
        col = lax.broadcasted_iota(jnp.int32, (n_rows, PAGE_SIZE), 1)
        process(kn_ref[0], vn_ref[0], col <= row % n_tok)

        lam = _diff_lambda(lam_ref, lam_init)
        an = acc_scr[...] / l_scr[...]
        d = an[0:half] - lam * an[half:n_rows]
        rowh = lax.broadcasted_iota(jnp.int32, d.shape, 0) // n_tok
        colh = lax.broadcasted_iota(jnp.int32, d.shape, 1) // V_DIM
        d = jnp.where(rowh == colh, d, 0.0)
        o = d[:, 0:V_DIM]
        for h in range(1, N_HEADS):
            o = o + d[:, h * V_DIM:(h + 1) * V_DIM]
        o_ref[0] = _head_rms(o, g_ref[...], lam_init).astype(BF16)


def paged_diff_attention(layer, page_table, qbd, cache_k, cache_v, k_new, v_new, n_tok, lam_vecs, subln_g, lam_init):
    nb, n_pages = page_table.shape
    n_rows = qbd.shape[1]
    kern = functools.partial(_decode_kernel, n_pages=n_pages, n_tok=n_tok, lam_init=lam_init)
    page_idx = lambda b, p, pt: (layer, pt[b * n_pages + p], 0, 0)
    grid_spec = pltpu.PrefetchScalarGridSpec(
        num_scalar_prefetch=1,
        grid=(nb, n_pages),
        in_specs=[
            pl.BlockSpec((1, n_rows, D_MODEL), lambda b, p, pt: (b, 0, 0)),
            pl.BlockSpec((1, 1, PAGE_SIZE, D_MODEL), page_idx),
            pl.BlockSpec((1, 1, PAGE_SIZE, D_MODEL), page_idx),
            pl.BlockSpec((1, PAGE_SIZE, D_MODEL), lambda b, p, pt: (b, 0, 0)),
            pl.BlockSpec((1, PAGE_SIZE, D_MODEL), lambda b, p, pt: (b, 0, 0)),
            pl.BlockSpec((4, HEAD_DIM), lambda b, p, pt: (0, 0)),
            pl.BlockSpec((1, V_DIM), lambda b, p, pt: (0, 0)),
        ],
        out_specs=pl.BlockSpec((1, n_rows // 2, V_DIM), lambda b, p, pt: (b, 0, 0)),
        scratch_shapes=[
            pltpu.VMEM((n_rows, 1), F32),
            pltpu.VMEM((n_rows, 1), F32),
            pltpu.VMEM((n_rows, D_MODEL), F32),
        ],
    )
    return pl.pallas_call(
        kern,
        out_shape=jax.ShapeDtypeStruct((nb, n_rows // 2, V_DIM), BF16),
        grid_spec=grid_spec,
        compiler_params=_params("arbitrary", "arbitrary"),
        name="paged_diff_attention",
    )(page_table.reshape(-1), qbd, cache_k, cache_v, k_new, v_new, lam_vecs, subln_g)


def _split_bf16(a):
    hi = a.astype(BF16)
    lo = (a - hi.astype(F32)).astype(BF16)
    return hi, lo


def _router_kernel(x_ref, sh_ref, sc_ref, w_ref, b_ref, h_ref, r_ref):
    h = x_ref[...] * (1.0 + sc_ref[0]) + sh_ref[0]
    h_ref[...] = h
    hh, hl = _split_bf16(h)
    wh, wl = _split_bf16(w_ref[...])
    logits = _dot(hh, wh) + _dot(hl, wh) + _dot(hh, wl) + b_ref[...]

    lane = lax.broadcasted_iota(jnp.int32, logits.shape, 1).astype(F32)

    def first_max(vals):
        vmax = jnp.max(vals, axis=1, keepdims=True)
        idx = jnp.min(jnp.where(vals == vmax, lane, float(ROUTE_LANES)), axis=1, keepdims=True)
        return vmax, idx

    gl = jnp.where(lane < N_GROUPS, logits, NEG_INF)
    gmax, gsel = first_max(gl)
    pg_sel = 1.0 / jnp.sum(jnp.exp(gl - gmax), axis=1, keepdims=True)

    lo = N_GROUPS + EXPERTS_PER_GROUP * gsel
    el = jnp.where((lane >= lo) & (lane < lo + EXPERTS_PER_GROUP), logits, NEG_INF)
    v0, i0 = first_max(el)
    v1, i1 = first_max(jnp.where(lane == i0, NEG_INF, el))
    e1 = jnp.exp(v1 - v0)
    w0 = pg_sel / (1.0 + e1)
    w1 = pg_sel * e1 / (1.0 + e1)

    out = jnp.where(lane == 0, i0 - N_GROUPS, 0.0)
    out = jnp.where(lane == 1, i1 - N_GROUPS, out)
    out = jnp.where(lane == 2, w0, out)
    out = jnp.where(lane == 3, w1, out)
    r_ref[...] = out


def moe_router(seg, x, shift, scale, w_route, b_route):
    row = lambda i: (0, 0)
    return pl.pallas_call(
        _router_kernel,
        out_shape=(jax.ShapeDtypeStruct((seg.rows, D_MODEL), F32),
                   jax.ShapeDtypeStruct((seg.rows, ROUTE_LANES), F32)),
        grid=(seg.nt,),
        in_specs=[
            pl.BlockSpec((seg.tm, D_MODEL), lambda i: (i, 0)),
            pl.BlockSpec(seg.mod_block(), seg.mod_index),
            pl.BlockSpec(seg.mod_block(), seg.mod_index),
            pl.BlockSpec((D_MODEL, ROUTE_LANES), row),
            pl.BlockSpec((1, ROUTE_LANES), row),
        ],
        out_specs=(pl.BlockSpec((seg.tm, D_MODEL), lambda i: (i, 0)),
                   pl.BlockSpec((seg.tm, ROUTE_LANES), lambda i: (i, 0))),
        compiler_params=_params("arbitrary"),
        name="moe_router",
    )(x, shift, scale, w_route, b_route)


def route_layout(expert_ids, n_tiles):
    n_assign = expert_ids.size
    e = expert_ids.reshape(n_assign)
    onehot = (e[:, None] == jnp.arange(N_EXPERTS, dtype=jnp.int32)[None, :]).astype(jnp.int32)
    csum = jnp.cumsum(onehot, axis=0)
    rank = jnp.sum(csum * onehot, axis=1) - 1
    counts = csum[-1]
    padded = ((counts + MOE_TM - 1) // MOE_TM) * MOE_TM
    ends = jnp.cumsum(padded)
    starts = ends - padded
    pos = starts[e] + rank
    src_rows = jnp.zeros((n_tiles * MOE_TM,), jnp.int32).at[pos].set(jnp.arange(n_assign, dtype=jnp.int32) // TOP_K)
    tile_start = jnp.arange(n_tiles, dtype=jnp.int32) * MOE_TM
    valid = tile_start < ends[-1]
    tile_e = jnp.minimum(jnp.searchsorted(ends, tile_start, side="right"), N_EXPERTS - 1).astype(jnp.int32)
    last_e = jnp.max(jnp.where(valid, tile_e, 0))
    tile_e = jnp.where(valid, tile_e, last_e)
    prev_e = jnp.concatenate([jnp.full((1,), -1, jnp.int32), tile_e[:-1]])
    first = valid & (tile_e != prev_e)
    return pos.astype(jnp.int32), src_rows, tile_e, valid.astype(jnp.int32), first.astype(jnp.int32)


def _moe_kernel(te_ref, tv_ref, tf_ref, src_ref, h_hbm, win_ref, wout_ref, y_ref, buf, sem, win_b, wout_b, *, n_tiles):
    i = pl.program_id(0)
    slot = i % 2

    def gather(tile, dst_slot):
        def body(r, carry):
            src = src_ref[tile * MOE_TM + r]
            pltpu.make_async_copy(h_hbm.at[pl.ds(src, 1)], buf.at[dst_slot, pl.ds(r, 1)], sem.at[dst_slot]).start()
            return carry
        lax.fori_loop(0, MOE_TM, body, 0)

    @pl.when(i == 0)
    def _():
        gather(0, 0)

    @pl.when((i + 1 < n_tiles) & (tv_ref[jnp.minimum(i + 1, n_tiles - 1)] == 1))
    def _():
        gather(i + 1, 1 - slot)

    @pl.when(tf_ref[i] == 1)
    def _():
        win_b[...] = win_ref[0, 0].astype(BF16)
        wout_b[...] = wout_ref[0, 0].astype(BF16)

    @pl.when(tv_ref[i] == 1)
    def _():
        pltpu.make_async_copy(h_hbm.at[pl.ds(0, MOE_TM)], buf.at[slot], sem.at[slot]).wait()
        u = _dot(buf[slot].astype(BF16), win_b[...])
        a = u[:, :D_EXPERT]
        act = a * jax.nn.sigmoid(a) * u[:, D_EXPERT:]
        y_ref[...] = _dot(act.astype(BF16), wout_b[...])

    @pl.when(tv_ref[i] == 0)
    def _():
        y_ref[...] = jnp.zeros_like(y_ref)


def moe_experts(layer, tile_e, tile_valid, tile_first, src_rows, h_all, w_in, w_out):
    n_tiles = tile_e.shape[0]
    kern = functools.partial(_moe_kernel, n_tiles=n_tiles)
    grid_spec = pltpu.PrefetchScalarGridSpec(
        num_scalar_prefetch=4,
        grid=(n_tiles,),
        in_specs=[
            pl.BlockSpec(memory_space=pl.ANY),
            pl.BlockSpec((1, 1, D_MODEL, 2 * D_EXPERT), lambda i, te, tv, tf, sr: (layer, te[i], 0, 0)),
            pl.BlockSpec((1, 1, D_EXPERT, D_MODEL), lambda i, te, tv, tf, sr: (layer, te[i], 0, 0)),
        ],
        out_specs=pl.BlockSpec((MOE_TM, D_MODEL), lambda i, te, tv, tf, sr: (i, 0)),
        scratch_shapes=[
            pltpu.VMEM((2, MOE_TM, D_MODEL), F32),
            pltpu.SemaphoreType.DMA((2,)),
            pltpu.VMEM((D_MODEL, 2 * D_EXPERT), BF16),
            pltpu.VMEM((D_EXPERT, D_MODEL), BF16),
        ],
    )
    return pl.pallas_call(
        kern,
        out_shape=jax.ShapeDtypeStruct((n_tiles * MOE_TM, D_MODEL), F32),
        grid_spec=grid_spec,
        compiler_params=_params("arbitrary"),
        name="moe_experts",
    )(tile_e, tile_valid, tile_first, src_rows, h_all, w_in, w_out)


def _combine_kernel(pos_ref, y_hbm, r_ref, x_ref, gate_ref, g_ref, b_ref, o_ref, buf, sem, *, tm, nt):
    i = pl.program_id(0)
    slot = i % 2

    def gather(tile, dst_slot):
        def body(r, carry):
            for k in range(TOP_K):
                src = pos_ref[(tile * tm + r) * TOP_K + k]
                pltpu.make_async_copy(y_hbm.at[pl.ds(src, 1)], buf.at[dst_slot, k, pl.ds(r, 1)], sem.at[dst_slot]).start()
            return carry
        lax.fori_loop(0, tm, body, 0)

    @pl.when(i == 0)
    def _():
        gather(0, 0)

    @pl.when(i + 1 < nt)
    def _():
        gather(i + 1, 1 - slot)

    for k in range(TOP_K):
        pltpu.make_async_copy(y_hbm.at[pl.ds(0, tm)], buf.at[slot, k], sem.at[slot]).wait()
    r = r_ref[...]
    moe = r[:, 2:3] * buf[slot, 0] + r[:, 3:4] * buf[slot, 1]
    z = ALPHA * x_ref[...] + gate_ref[0] * moe
    o_ref[...] = _layer_norm(z, g_ref[...], b_ref[...])


def moe_combine_postnorm(seg, pos, y_sorted, route, x, gate, ln_g, ln_b):
    kern = functools.partial(_combine_kernel, tm=seg.tm, nt=seg.nt)
    row = lambda i, ps: (0, 0)
    grid_spec = pltpu.PrefetchScalarGridSpec(
        num_scalar_prefetch=1,
        grid=(seg.nt,),
        in_specs=[
            pl.BlockSpec(memory_space=pl.ANY),
            pl.BlockSpec((seg.tm, ROUTE_LANES), lambda i, ps: (i, 0)),
            pl.BlockSpec((seg.tm, D_MODEL), lambda i, ps: (i, 0)),
            pl.BlockSpec(seg.mod_block(), lambda i, ps: seg.mod_index(i)),
            pl.BlockSpec((1, D_MODEL), row),
            pl.BlockSpec((1, D_MODEL), row),
        ],
        out_specs=pl.BlockSpec((seg.tm, D_MODEL), lambda i, ps: (i, 0)),
        scratch_shapes=[pltpu.VMEM((2, TOP_K, seg.tm, D_MODEL), F32), pltpu.SemaphoreType.DMA((2,))],
    )
    return pl.pallas_call(
        kern,
        out_shape=jax.ShapeDtypeStruct((seg.rows, D_MODEL), F32),
        grid_spec=grid_spec,
        compiler_params=_params("arbitrary"),
        name="moe_combine_postnorm",
    )(pos, y_sorted, route, x, gate, ln_g, ln_b)


def _rope_tables(pos):
    half = HEAD_DIM // 2
    inv = ROPE_THETA ** (-jnp.arange(half, dtype=F32) / half)
    ang = pos.astype(F32)[:, None] * inv[None, :]
    cos = jnp.cos(ang)
    sin = jnp.sin(ang)
    return jnp.concatenate([cos, cos], axis=-1), jnp.concatenate([-sin, sin], axis=-1)


def kernel(x_prompt, x_sample, cache_k, cache_v, state_conv, page_table, c_prompt, c_sample, ada_w, ada_b, ln_g, ln_b, conv_w_in, conv_b_in, conv_dw, conv_dw_b, conv_ln_g, conv_ln_b, conv_w_out, conv_b_out, attn_w_qkv, attn_lambda, attn_subln_g, attn_w_out, moe_w_group, moe_b_group, moe_w_expert, moe_b_expert, moe_w_in, moe_w_out):
    bp, tp, d = x_prompt.shape
    bs, ts, _ = x_sample.shape
    n_pages = page_table.shape[1]
    past_len = n_pages * PAGE_SIZE
    mp, ms = bp * tp, bs * ts
    assert d == D_MODEL and tp % 512 == 0 and ms % 8 == 0

    seg_p = Seg(mp, 512, tp, per_row=False)
    seg_p256 = Seg(mp, 256, tp, per_row=False)
    seg_s = Seg(ms, ms, ts, per_row=True)

    n_c = bp + bs
    c_rows = -(-n_c // 8) * 8
    c_pad = jnp.concatenate([c_prompt, c_sample, jnp.zeros((c_rows - n_c, d), F32)], axis=0)
    mods = ada_all(c_pad, ada_w.reshape(DEPTH * 2, d, 3 * d), ada_b.reshape(DEPTH * 2, 1, 3 * d))

    def mod_params(i, sub):
        m = mods[i * 2 + sub]
        out_p = [m[:bp, c * d:(c + 1) * d].reshape(bp, 1, d) for c in range(3)]
        out_s = [jnp.repeat(m[bp:n_c, c * d:(c + 1) * d], ts, axis=0).reshape(1, ms, d) for c in range(3)]
        return out_p, out_s

    cos_p, sin_p = _rope_tables(jnp.arange(tp, dtype=jnp.int32))
    cos_s, sin_s = _rope_tables(jnp.tile(past_len + jnp.arange(ts, dtype=jnp.int32), bs))

    x_p = x_prompt.reshape(mp, d)
    x_s = x_sample.reshape(ms, d)
    pool = cache_k.shape[1]
    cache_k2 = cache_k.reshape(cache_k.shape[0], pool, PAGE_SIZE, d)
    cache_v2 = cache_v.reshape(cache_v.shape[0], pool, PAGE_SIZE, d)

    n_assign = (mp + ms) * TOP_K
    n_tiles = -(-n_assign // MOE_TM) + N_EXPERTS - 1

    k_p_rows, v_p_rows, conv_p_rows = [], [], []
    k_s_rows, v_s_rows, conv_s_rows = [], [], []
    row2 = lambda a: a.reshape(1, -1)

    for i in range(DEPTH):
        l = i // N_MIXERS
        (sh_p, sc_p, gt_p), (sh_s, sc_s, gt_s) = mod_params(i, 0)
        lg, lb = row2(ln_g[i, 0]), row2(ln_b[i, 0])
        if i % N_MIXERS == 0:
            w_in = conv_w_in[l].astype(BF16)
            b_in = row2(conv_b_in[l])
            w_out = conv_w_out[l].astype(BF16)
            b_out = row2(conv_b_out[l])
            dw = jnp.concatenate([conv_dw[l], jnp.zeros((HALO - CONV_WIDTH, d), F32)], axis=0)
            cw = (dw, row2(conv_dw_b[l]), row2(conv_ln_g[l]), row2(conv_ln_b[l]))

            glu_p = mod_mm_glu(seg_p, x_p, sh_p, sc_p, w_in, b_in).reshape(bp, tp, d)
            tt = 128
            halo_idx = lambda b, t: (b, jnp.maximum(t * (tt // HALO) - 1, 0), 0)
            y_p = dwconv_ln_silu(glu_p, glu_p, halo_idx, True, tt, *cw).reshape(mp, d)
            conv_p_rows.append(glu_p[:, tp - (CONV_WIDTH - 1):])
            x_p = mm_postnorm(seg_p256, y_p, w_out, b_out, x_p, gt_p, lg, lb)

            glu_s = mod_mm_glu(seg_s, x_s, sh_s, sc_s, w_in, b_in).reshape(bs, ts, d)
            hist = state_conv[l].astype(F32)
            halo_s = jnp.concatenate([jnp.zeros((bs, HALO - (CONV_WIDTH - 1), d), F32), hist], axis=1)
            cur_s = jnp.concatenate([glu_s, jnp.zeros((bs, 8 - ts, d), F32)], axis=1)
            y_s = dwconv_ln_silu(cur_s, halo_s, lambda b, t: (b, 0, 0), False, 8, *cw)[:, :ts].reshape(ms, d)
            conv_s_rows.append(jnp.concatenate([hist, glu_s], axis=1)[:, ts:])
            x_s = mm_postnorm(seg_s, y_s, w_out, b_out, x_s, gt_s, lg, lb)
        else:
            lam_init = 0.8 - 0.6 * math.exp(-0.3 * i)
            w_qkv = attn_w_qkv[l].astype(BF16)
            w_out = attn_w_out[l].astype(BF16)
            no_bias = jnp.zeros((1, d), F32)
            lam_vecs = attn_lambda[l].astype(F32)
            sub_g = row2(attn_subln_g[l])

            qkv_p, qkv_p_bf = mod_mm_qkv(seg_p, x_p, sh_p, sc_p, w_qkv, cos_p, sin_p)
            k_p_rows.append(qkv_p[1].reshape(bp, tp, N_HEADS, 2, HEAD_DIM))
            v_p_rows.append(qkv_p[2].reshape(bp, tp, N_HEADS, V_DIM))
            o_p = flash_diff_attention(qkv_p_bf, bp, tp, lam_vecs, sub_g, lam_init)
            x_p = mm_postnorm(seg_p256, o_p, w_out, no_bias, x_p, gt_p, lg, lb)

            qkv_s, qkv_s_bf = mod_mm_qkv(seg_s, x_s, sh_s, sc_s, w_qkv, cos_s, sin_s)
            k_s_rows.append(qkv_s[1].reshape(bs, ts, N_HEADS, 2, HEAD_DIM))
            v_s_rows.append(qkv_s[2].reshape(bs, ts, N_HEADS, V_DIM))
            q5 = qkv_s_bf[0].reshape(bs, ts, N_HEADS, 2, HEAD_DIM)
            eye_h = jnp.eye(N_HEADS, dtype=BF16)
            eye_j = jnp.eye(2, dtype=BF16)
            qbd = jnp.einsum("bthjd,hH,jJ->bjhtHJd", q5, eye_h, eye_j).reshape(bs, 2 * N_HEADS * ts, d)
            pad_rows = lambda a: jnp.concatenate(
                [a.reshape(bs, ts, d), jnp.zeros((bs, PAGE_SIZE - ts, d), BF16)], axis=1)
            o_s = paged_diff_attention(l, page_table, qbd, cache_k2, cache_v2, pad_rows(qkv_s_bf[1]),
                                       pad_rows(qkv_s_bf[2]), ts, lam_vecs, sub_g, lam_init)
            o_s = o_s.reshape(bs, N_HEADS, ts, V_DIM).transpose(0, 2, 1, 3).reshape(ms, d)
            x_s = mm_postnorm(seg_s, o_s, w_out, no_bias, x_s, gt_s, lg, lb)

        (sh_p, sc_p, gt_p), (sh_s, sc_s, gt_s) = mod_params(i, 1)
        lg, lb = row2(ln_g[i, 1]), row2(ln_b[i, 1])
        n_logit = N_GROUPS + N_EXPERTS
        w_route = jnp.concatenate([moe_w_group[i], moe_w_expert[i], jnp.zeros((d, ROUTE_LANES - n_logit), F32)], axis=1)
        b_route = jnp.concatenate([moe_b_group[i], moe_b_expert[i], jnp.zeros((ROUTE_LANES - n_logit,), F32)]).reshape(1, -1)
        h_p, route_p = moe_router(seg_p, x_p, sh_p, sc_p, w_route, b_route)
        h_s, route_s = moe_router(seg_s, x_s, sh_s, sc_s, w_route, b_route)
        h_all = jnp.concatenate([h_p, h_s], axis=0)
        ids = jnp.concatenate([route_p[:, :TOP_K], route_s[:, :TOP_K]], axis=0).astype(jnp.int32)
        pos, src_rows, tile_e, tile_valid, tile_first = route_layout(ids, n_tiles)
        y_sorted = moe_experts(i, tile_e, tile_valid, tile_first, src_rows, h_all, moe_w_in, moe_w_out)
        x_p = moe_combine_postnorm(seg_p256, pos[:mp * TOP_K], y_sorted, route_p, x_p, gt_p, lg, lb)
        x_s = moe_combine_postnorm(seg_s, pos[mp * TOP_K:], y_sorted, route_s, x_s, gt_s, lg, lb)

    return (x_p.reshape(bp, tp, d), x_s.reshape(bs, ts, d),
            jnp.stack(k_p_rows), jnp.stack(v_p_rows), jnp.stack(conv_p_rows),
            jnp.stack(k_s_rows), jnp.stack(v_s_rows), jnp.stack(conv_s_rows))
```

```python
import functools
import math

import jax
import jax.numpy as jnp
from jax import lax
from jax.experimental import pallas as pl
from jax.experimental.pallas import tpu as pltpu

D_MODEL = 2048
DEPTH = 4
PAGE_SIZE = 128
N_MIXERS = 2
CONV_WIDTH = 31
N_HEADS = 8
HEAD_DIM = D_MODEL // (2 * N_HEADS)
V_DIM = 2 * HEAD_DIM
ROPE_THETA = 10000.0
N_GROUPS = 4
EXPERTS_PER_GROUP = 8
N_EXPERTS = N_GROUPS * EXPERTS_PER_GROUP
TOP_K = 2
D_EXPERT = D_MODEL // 4
ALPHA = (2 * DEPTH) ** 0.25
LN_EPS = 1e-5

F32 = jnp.float32
BF16 = jnp.bfloat16
LANES = 128
HALO = 32
VMEM_LIMIT = 56 * 1024 * 1024
MOE_TM = 256
ROUTE_LANES = 128
NEG_INF = float("-inf")
Q_SCALE = HEAD_DIM ** -0.5 * math.log2(math.e)
DEC_PAGES = 4
Q_ROWS = 8


def _params(*sem):
    return pltpu.CompilerParams(dimension_semantics=sem, vmem_limit_bytes=VMEM_LIMIT)


def _dot(a, b):
    return jnp.dot(a, b, preferred_element_type=F32)


def _dot_nt(a, b):
    return lax.dot_general(a, b, (((1,), (1,)), ((), ())), preferred_element_type=F32)


def _layer_norm(z, g, b):
    mu = jnp.mean(z, axis=-1, keepdims=True)
    zc = z - mu
    var = jnp.mean(zc * zc, axis=-1, keepdims=True)
    return zc * lax.rsqrt(var + LN_EPS) * g + b


class Seg:
    def __init__(self, rows, tm, rows_per_batch, per_row):
        self.rows = rows
        self.tm = tm
        self.nt = rows // tm
        self.per_row = per_row
        self.tiles_per_batch = max(rows_per_batch // tm, 1)

    def mod_block(self):
        return (1, self.tm if self.per_row else 1, D_MODEL)

    def mod_index(self, i):
        return (0 if self.per_row else i // self.tiles_per_batch, 0, 0)

    def pos_index(self, i):
        return (0 if self.per_row else i % self.tiles_per_batch, 0)


def _ada_kernel(c_ref, w_ref, b_ref, o_ref):
    o_ref[0] = _dot(c_ref[...].astype(BF16), w_ref[0].astype(BF16)) + b_ref[0]


def ada_all(c_pad, w, b):
    n_sub, _, n_out = w.shape
    rows = c_pad.shape[0]
    tn = 1024
    return pl.pallas_call(
        _ada_kernel,
        out_shape=jax.ShapeDtypeStruct((n_sub, rows, n_out), F32),
        grid=(n_sub, n_out // tn),
        in_specs=[
            pl.BlockSpec((rows, D_MODEL), lambda s, j: (0, 0)),
            pl.BlockSpec((1, D_MODEL, tn), lambda s, j: (s, 0, j)),
            pl.BlockSpec((1, 1, tn), lambda s, j: (s, 0, j)),
        ],
        out_specs=pl.BlockSpec((1, rows, tn), lambda s, j: (s, 0, j)),
        compiler_params=_params("arbitrary", "arbitrary"),
        name="ada_all",
    )(c_pad, w, b)


def _glu_kernel(x_ref, sh_ref, sc_ref, wa_ref, wg_ref, ba_ref, bg_ref, o_ref, h_scr):
    @pl.when(pl.program_id(1) == 0)
    def _():
        h_scr[...] = (x_ref[...] * (1.0 + sc_ref[0]) + sh_ref[0]).astype(BF16)

    h = h_scr[...]
    a = _dot(h, wa_ref[...]) + ba_ref[...]
    g = _dot(h, wg_ref[...]) + bg_ref[...]
    o_ref[...] = a * jax.nn.sigmoid(g)


def mod_mm_glu(seg, x, shift, scale, w_in, b_in):
    tn = 512
    nj = D_MODEL // tn
    return pl.pallas_call(
        _glu_kernel,
        out_shape=jax.ShapeDtypeStruct((seg.rows, D_MODEL), F32),
        grid=(seg.nt, nj),
        in_specs=[
            pl.BlockSpec((seg.tm, D_MODEL), lambda i, j: (i, 0)),
            pl.BlockSpec(seg.mod_block(), lambda i, j: seg.mod_index(i)),
            pl.BlockSpec(seg.mod_block(), lambda i, j: seg.mod_index(i)),
            pl.BlockSpec((D_MODEL, tn), lambda i, j: (0, j)),
            pl.BlockSpec((D_MODEL, tn), lambda i, j: (0, j + nj)),
            pl.BlockSpec((1, tn), lambda i, j: (0, j)),
            pl.BlockSpec((1, tn), lambda i, j: (0, j + nj)),
        ],
        out_specs=pl.BlockSpec((seg.tm, tn), lambda i, j: (i, j)),
        scratch_shapes=[pltpu.VMEM((seg.tm, D_MODEL), BF16)],
        compiler_params=_params("arbitrary", "arbitrary"),
        name="mod_mm_glu",
    )(x, shift, scale, w_in, w_in, b_in, b_in)


def _dwconv_kernel(cur_ref, halo_ref, dw_ref, dwb_ref, g_ref, b_ref, o_ref, xs_scr, sh_scr, y_scr, *, tt, zero_first):
    hist = halo_ref[0]
    if zero_first:
        hist = jnp.where(pl.program_id(1) == 0, jnp.zeros_like(hist), hist)
    xs_scr[0:HALO, :] = hist
    xs_scr[HALO:HALO + tt, :] = cur_ref[0]

    rch = min(tt, 32)
    cch = sh_scr.shape[2]
    span = sh_scr.shape[1]
    lead = HALO - (CONV_WIDTH - 1)

    def col_body(cc, carry):
        cols = pl.ds(pl.multiple_of(cc * cch, cch), cch)
        for r in range(1, 8):
            sh_scr[r - 1] = xs_scr[r:r + span, cols]
        for rc in range(tt // rch):
            acc = jnp.zeros((rch, cch), F32)
            for j in range(CONV_WIDTH):
                q, r = divmod(lead + j, 8)
                u0 = rc * rch + 8 * q
                win = xs_scr[u0:u0 + rch, cols] if r == 0 else sh_scr[r - 1, u0:u0 + rch, :]
                acc = acc + dw_ref[j:j + 1, cols] * win
            y_scr[rc * rch:(rc + 1) * rch, cols] = acc + dwb_ref[:, cols]
        return carry

    lax.fori_loop(0, D_MODEL // cch, col_body, 0)
    y = _layer_norm(y_scr[...], g_ref[...], b_ref[...])
    o_ref[0] = (y * jax.nn.sigmoid(y)).astype(BF16)


def dwconv_ln_silu(cur, halo, halo_index, zero_first, tt, dw, dw_b, ln_g, ln_b):
    nb, t, _ = cur.shape
    kern = functools.partial(_dwconv_kernel, tt=tt, zero_first=zero_first)
    return pl.pallas_call(
        kern,
        out_shape=jax.ShapeDtypeStruct((nb, t, D_MODEL), BF16),
        grid=(nb, t // tt),
        in_specs=[
            pl.BlockSpec((1, tt, D_MODEL), lambda b, i: (b, i, 0)),
            pl.BlockSpec((1, HALO, D_MODEL), halo_index),
            pl.BlockSpec((HALO, D_MODEL), lambda b, i: (0, 0)),
            pl.BlockSpec((1, D_MODEL), lambda b, i: (0, 0)),
            pl.BlockSpec((1, D_MODEL), lambda b, i: (0, 0)),
            pl.BlockSpec((1, D_MODEL), lambda b, i: (0, 0)),
        ],
        out_specs=pl.BlockSpec((1, tt, D_MODEL), lambda b, i: (b, i, 0)),
        scratch_shapes=[pltpu.VMEM((HALO + tt, D_MODEL), F32),
                        pltpu.VMEM((7, tt + HALO - 8, 256), F32),
                        pltpu.VMEM((tt, D_MODEL), F32)],
        compiler_params=_params("arbitrary", "arbitrary"),
        name="dwconv_ln_silu",
    )(cur, halo, dw, dw_b, ln_g, ln_b)


def _mm_postnorm_kernel(y_ref, w_ref, bias_ref, x_ref, gate_ref, g_ref, b_ref, o_ref):
    out = _dot(y_ref[...], w_ref[...]) + bias_ref[...]
    z = ALPHA * x_ref[...] + gate_ref[0] * out
    o_ref[...] = _layer_norm(z, g_ref[...], b_ref[...])


def mm_postnorm(seg, y, w, bias, x, gate, ln_g, ln_b):
    k = y.shape[1]
    row = lambda i: (0, 0)
    return pl.pallas_call(
        _mm_postnorm_kernel,
        out_shape=jax.ShapeDtypeStruct((seg.rows, D_MODEL), F32),
        grid=(seg.nt,),
        in_specs=[
            pl.BlockSpec((seg.tm, k), lambda i: (i, 0)),
            pl.BlockSpec((k, D_MODEL), row),
            pl.BlockSpec((1, D_MODEL), row),
            pl.BlockSpec((seg.tm, D_MODEL), lambda i: (i, 0)),
            pl.BlockSpec(seg.mod_block(), seg.mod_index),
            pl.BlockSpec((1, D_MODEL), row),
            pl.BlockSpec((1, D_MODEL), row),
        ],
        out_specs=pl.BlockSpec((seg.tm, D_MODEL), lambda i: (i, 0)),
        compiler_params=_params("arbitrary"),
        name="mm_postnorm",
    )(y, w, bias, x, gate, ln_g, ln_b)


def _qkv_kernel(x_ref, sh_ref, sc_ref, w_ref, cos_ref, sin_ref, *rest, tm, tn, slab):
    ok_ref, ov_ref, ob_ref, h_scr = rest[-4:]
    part = pl.program_id(1)
    col_tile = pl.program_id(2)
    n_grp = tn // HEAD_DIM
    n_hj = D_MODEL // HEAD_DIM

    @pl.when((part == 0) & (col_tile == 0))
    def _():
        h_scr[...] = (x_ref[...] * (1.0 + sc_ref[0]) + sh_ref[0]).astype(BF16)

    acc = _dot(h_scr[...], w_ref[...])

    @pl.when(part < 2)
    def _():
        cos = cos_ref[...]
        sin = sin_ref[...]
        rot = []
        for g in range(n_grp):
            blk = acc[:, g * HEAD_DIM:(g + 1) * HEAD_DIM]
            rot.append(blk * cos + pltpu.roll(blk, HEAD_DIM // 2, 1) * sin)
        qscale = jnp.where(part == 0, Q_SCALE, 1.0).astype(F32)
        for g in range(n_grp):
            ob_ref[0, :, g * HEAD_DIM:(g + 1) * HEAD_DIM] = (rot[g] * qscale).astype(BF16)
        for jj in range(D_MODEL // tn):
            @pl.when((part == 1) & (col_tile == jj))
            def _():
                for g in range(n_grp):
                    ok_ref[slab, pl.ds(jj * n_grp + g, tm, stride=n_hj), :] = rot[g]
                if jj == 0:
                    for other in range(ok_ref.shape[0]):
                        if other != slab:
                            ok_ref[other] = jnp.zeros(ok_ref.shape[1:], F32)

    @pl.when(part == 2)
    def _():
        ob_ref[0] = acc.astype(BF16)
        ov_ref[slab] = acc
        for other in range(ov_ref.shape[0]):
            if other != slab:
                ov_ref[other] = jnp.zeros(ov_ref.shape[1:], F32)


def mod_mm_qkv(seg, layer, n_layers, x, shift, scale, w_qkv, cos, sin, prev=None):
    tn = 512
    nj = D_MODEL // tn
    n_hj = D_MODEL // HEAD_DIM
    aliased = prev is not None
    n_slabs, slab, lead = (1, 0, layer) if aliased else (n_layers, layer, 0)
    kern = functools.partial(_qkv_kernel, tm=seg.tm, tn=tn, slab=slab)
    in_specs = [
        pl.BlockSpec((seg.tm, D_MODEL), lambda i, p, j: (i, 0)),
        pl.BlockSpec(seg.mod_block(), lambda i, p, j: seg.mod_index(i)),
        pl.BlockSpec(seg.mod_block(), lambda i, p, j: seg.mod_index(i)),
        pl.BlockSpec((D_MODEL, tn), lambda i, p, j: (0, p * nj + j)),
        pl.BlockSpec((seg.tm, HEAD_DIM), lambda i, p, j: seg.pos_index(i)),
        pl.BlockSpec((seg.tm, HEAD_DIM), lambda i, p, j: seg.pos_index(i)),
    ]
    args = [x, shift, scale, w_qkv, cos, sin]
    aliases = {}
    if aliased:
        in_specs += [pl.BlockSpec(memory_space=pl.ANY), pl.BlockSpec(memory_space=pl.ANY)]
        aliases = {len(args): 0, len(args) + 1: 1}
        args += list(prev)
    return pl.pallas_call(
        kern,
        out_shape=(jax.ShapeDtypeStruct((n_layers, seg.rows * n_hj, HEAD_DIM), F32),
                   jax.ShapeDtypeStruct((n_layers, seg.rows, D_MODEL), F32),
                   jax.ShapeDtypeStruct((3, seg.rows, D_MODEL), BF16)),
        grid=(seg.nt, 3, nj),
        in_specs=in_specs,
        out_specs=(pl.BlockSpec((n_slabs, seg.tm * n_hj, HEAD_DIM), lambda i, p, j: (lead, i, 0)),
                   pl.BlockSpec((n_slabs, seg.tm, tn), lambda i, p, j: (lead, i, jnp.where(p == 2, j, 0))),
                   pl.BlockSpec((1, seg.tm, tn), lambda i, p, j: (p, i, j))),
        scratch_shapes=[pltpu.VMEM((seg.tm, D_MODEL), BF16)],
        input_output_aliases=aliases,
        compiler_params=_params("arbitrary", "arbitrary", "arbitrary"),
        name="mod_mm_qkv",
    )(*args)


def _diff_lambda(lam_ref, lam_init):
    lv = lam_ref[...]
    s01 = jnp.sum(lv[0:1] * lv[1:2], axis=-1, keepdims=True)
    s23 = jnp.sum(lv[2:3] * lv[3:4], axis=-1, keepdims=True)
    return jnp.exp(s01) - jnp.exp(s23) + lam_init


def _head_rms(o, g, lam_init):
    ms = jnp.mean(o * o, axis=-1, keepdims=True)
    return o * lax.rsqrt(ms + LN_EPS) * g * (1.0 - lam_init)


def _flash_kernel(qi_ref, ki_ref, q_ref, k_ref, v_ref, lam_ref, g_ref, o_ref, m_scr, l_scr, acc_scr, s_scr, p_scr, a_scr,
                  *, tq, lam_init):
    pr = pl.program_id(2)
    qi = qi_ref[pr]
    ki = ki_ref[pr]
    rc = 32
    n_kt = tq // LANES

    @pl.when(ki == 0)
    def _():
        m_scr[...] = jnp.full_like(m_scr, NEG_INF)
        l_scr[...] = jnp.zeros_like(l_scr)
        acc_scr[...] = jnp.zeros_like(acc_scr)

    def step(masked):
        for j in range(2):
            lanes = slice(j * HEAD_DIM, (j + 1) * HEAD_DIM)
            s_scr[j] = _dot_nt(q_ref[0, :, lanes], k_ref[0, :, lanes])
        for j in range(2):
            def chunk(c):
                r0 = c * rc
                rows = slice(r0, r0 + rc)
                parts = [s_scr[j, rows, kt * LANES:(kt + 1) * LANES] for kt in range(n_kt)]
                if masked:
                    row = r0 + lax.broadcasted_iota(jnp.int32, (rc, LANES), 0)
                    col = lax.broadcasted_iota(jnp.int32, (rc, LANES), 1)
                    parts = [jnp.where(col + kt * LANES <= row, parts[kt], NEG_INF) for kt in range(n_kt)]
                mx = parts[0]
                for kt in range(1, n_kt):
                    mx = jnp.maximum(mx, parts[kt])
                m_old = m_scr[j, rows, :]
                m_new = jnp.maximum(m_old, jnp.max(mx, axis=1, keepdims=True))
                alpha = jnp.exp2(m_old - m_new)
                lsum = jnp.zeros((rc, LANES), F32)
                for kt in range(n_kt):
                    p = jnp.exp2(parts[kt] - m_new)
                    lsum = lsum + p
                    p_scr[j, rows, kt * LANES:(kt + 1) * LANES] = p.astype(BF16)
                l_scr[j, rows, :] = alpha * l_scr[j, rows, :] + lsum
                m_scr[j, rows, :] = m_new
                a_scr[j, rows, :] = alpha

            for c in range(tq // rc):
                chunk(c)
            alpha = a_scr[j]
            acc_scr[j] = jnp.concatenate([alpha, alpha], axis=1) * acc_scr[j] + _dot(p_scr[j], v_ref[0])

    @pl.when(ki < qi)
    def _():
        step(False)

    @pl.when(ki == qi)
    def _():
        step(True)
        lam = _diff_lambda(lam_ref, lam_init)
        l0 = jnp.sum(l_scr[0], axis=1, keepdims=True)
        l1 = jnp.sum(l_scr[1], axis=1, keepdims=True)
        o = acc_scr[0] / l0 - lam * (acc_scr[1] / l1)
        o_ref[...] = _head_rms(o, g_ref[...], lam_init).astype(BF16)


def flash_diff_attention(qkv_bf, nb, t, lam_vecs, subln_g, lam_init):
    tq = 512
    nq = t // tq
    pairs = [(a, b) for a in range(nq) for b in range(a + 1)]
    qi_tab = jnp.asarray([p[0] for p in pairs], jnp.int32)
    ki_tab = jnp.asarray([p[1] for p in pairs], jnp.int32)
    kern = functools.partial(_flash_kernel, tq=tq, lam_init=lam_init)
    grid_spec = pltpu.PrefetchScalarGridSpec(
        num_scalar_prefetch=2,
        grid=(nb, N_HEADS, len(pairs)),
        in_specs=[
            pl.BlockSpec((1, tq, V_DIM), lambda b, h, p, qt, kt: (0, b * nq + qt[p], h)),
            pl.BlockSpec((1, tq, V_DIM), lambda b, h, p, qt, kt: (1, b * nq + kt[p], h)),
            pl.BlockSpec((1, tq, V_DIM), lambda b, h, p, qt, kt: (2, b * nq + kt[p], h)),
            pl.BlockSpec((4, HEAD_DIM), lambda b, h, p, qt, kt: (0, 0)),
            pl.BlockSpec((1, V_DIM), lambda b, h, p, qt, kt: (0, 0)),
        ],
        out_specs=pl.BlockSpec((tq, V_DIM), lambda b, h, p, qt, kt: (b * nq + qt[p], h)),
        scratch_shapes=[
            pltpu.VMEM((2, tq, LANES), F32),
            pltpu.VMEM((2, tq, LANES), F32),
            pltpu.VMEM((2, tq, V_DIM), F32),
            pltpu.VMEM((2, tq, tq), F32),
            pltpu.VMEM((2, tq, tq), BF16),
            pltpu.VMEM((2, tq, LANES), F32),
        ],
    )
    return pl.pallas_call(
        kern,
        out_shape=jax.ShapeDtypeStruct((nb * t, D_MODEL), BF16),
        grid_spec=grid_spec,
        compiler_params=_params("arbitrary", "arbitrary", "arbitrary"),
        name="flash_diff_attention",
    )(qi_tab, ki_tab, qkv_bf, qkv_bf, qkv_bf, lam_vecs, subln_g)


def _decode_kernel(pt_ref, q_ref, *refs, n_steps, lam_init):
    k_refs = refs[:DEC_PAGES]
    v_refs = refs[DEC_PAGES:2 * DEC_PAGES]
    kn_ref, vn_ref, spread_ref, lam_ref, g_ref, o_ref, m_scr, l_scr, acc_scr, own_scr = refs[2 * DEC_PAGES:]
    p = pl.program_id(1)
    n_hj = 2 * N_HEADS
    grp = 2 * Q_ROWS

    @pl.when(p == 0)
    def _():
        m_scr[...] = jnp.full_like(m_scr, NEG_INF)
        l_scr[...] = jnp.zeros_like(l_scr)
        acc_scr[...] = jnp.zeros_like(acc_scr)
        row_head = lax.broadcasted_iota(jnp.int32, own_scr.shape, 0) // grp
        col_head = lax.broadcasted_iota(jnp.int32, own_scr.shape, 1) % N_HEADS
        own_scr[...] = (row_head == col_head).astype(F32)

    def process(k_pages, v_pages, keep):
        q_parts = [q_ref[0, hj * Q_ROWS:(hj + 1) * Q_ROWS, :].astype(BF16) for hj in range(n_hj)]
        s_pages = []
        for k_ref in k_pages:
            parts = []
            for hj in range(n_hj):
                k_hj = k_ref[pl.ds(hj, PAGE_SIZE, stride=n_hj), :].astype(BF16)
                parts.append(_dot_nt(q_parts[hj], k_hj))
            s_pages.append(jnp.concatenate(parts, axis=0))
        s = jnp.concatenate(s_pages, axis=1)
        if keep is not None:
            s = jnp.where(keep, s, NEG_INF)
        m_old = m_scr[...]
        m_new = jnp.maximum(m_old, jnp.max(s, axis=1, keepdims=True))
        alpha = jnp.exp2(m_old - m_new)
        pe = jnp.exp2(s - m_new)
        l_scr[...] = alpha * l_scr[...] + jnp.sum(pe, axis=1, keepdims=True)
        m_scr[...] = m_new
        own = own_scr[...]
        pv = None
        for c, v_ref in enumerate(v_pages):
            pe_c = pe[:, c * PAGE_SIZE:(c + 1) * PAGE_SIZE].astype(BF16)
            spread = _dot(pe_c, spread_ref[...]) * own
            term = _dot(spread.astype(BF16), v_ref[...].astype(BF16))
            pv = term if pv is None else pv + term
        acc_scr[...] = alpha * acc_scr[...] + pv

    process([r.at[0, 0] for r in k_refs], [r.at[0, 0] for r in v_refs], None)

    @pl.when(p == n_steps - 1)
    def _():
        row = lax.broadcasted_iota(jnp.int32, (n_hj * Q_ROWS, PAGE_SIZE), 0)
        col = lax.broadcasted_iota(jnp.int32, (n_hj * Q_ROWS, PAGE_SIZE), 1)
        process([kn_ref.at[0]], [vn_ref.at[0]], col <= row % Q_ROWS)

        lam = _diff_lambda(lam_ref, lam_init)
        an = acc_scr[...] / l_scr[...]
        outs = []
        for h in range(N_HEADS):
            outs.append(an[h * grp:h * grp + Q_ROWS] - lam * an[h * grp + Q_ROWS:(h + 1) * grp])
        o = jnp.concatenate(outs, axis=0)
        o_ref[0] = _head_rms(o, g_ref[...], lam_init).astype(BF16)


def paged_diff_attention(layer, page_table, q_rows, cache_k, cache_v, k_new, v_new, lam_vecs, subln_g, lam_init):
    nb, n_pages = page_table.shape
    assert n_pages % DEC_PAGES == 0
    n_steps = n_pages // DEC_PAGES
    n_hj = 2 * N_HEADS
    kern = functools.partial(_decode_kernel, n_steps=n_steps, lam_init=lam_init)

    def page_idx(c):
        return lambda b, p, pt: (layer, pt[b * n_pages + p * DEC_PAGES + c], 0, 0)

    fixed = lambda b, p, pt: (b, 0, 0)
    grid_spec = pltpu.PrefetchScalarGridSpec(
        num_scalar_prefetch=1,
        grid=(nb, n_steps),
        in_specs=(
            [pl.BlockSpec((1, n_hj * Q_ROWS, HEAD_DIM), fixed)]
            + [pl.BlockSpec((1, 1, PAGE_SIZE * n_hj, HEAD_DIM), page_idx(c)) for c in range(DEC_PAGES)]
            + [pl.BlockSpec((1, 1, PAGE_SIZE * N_HEADS, V_DIM), page_idx(c)) for c in range(DEC_PAGES)]
            + [pl.BlockSpec((1, PAGE_SIZE * n_hj, HEAD_DIM), fixed),
               pl.BlockSpec((1, PAGE_SIZE * N_HEADS, V_DIM), fixed),
               pl.BlockSpec((PAGE_SIZE, PAGE_SIZE * N_HEADS), lambda b, p, pt: (0, 0)),
               pl.BlockSpec((4, HEAD_DIM), lambda b, p, pt: (0, 0)),
               pl.BlockSpec((1, V_DIM), lambda b, p, pt: (0, 0))]),
        out_specs=pl.BlockSpec((1, N_HEADS * Q_ROWS, V_DIM), fixed),
        scratch_shapes=[
            pltpu.VMEM((n_hj * Q_ROWS, 1), F32),
            pltpu.VMEM((n_hj * Q_ROWS, 1), F32),
            pltpu.VMEM((n_hj * Q_ROWS, V_DIM), F32),
            pltpu.VMEM((n_hj * Q_ROWS, PAGE_SIZE * N_HEADS), F32),
        ],
    )
    spread = jnp.repeat(jnp.eye(PAGE_SIZE, dtype=BF16), N_HEADS, axis=1)
    return pl.pallas_call(
        kern,
        out_shape=jax.ShapeDtypeStruct((nb, N_HEADS * Q_ROWS, V_DIM), BF16),
        grid_spec=grid_spec,
        compiler_params=_params("arbitrary", "arbitrary"),
        name="paged_diff_attention",
    )(page_table.reshape(-1), q_rows, *([cache_k] * DEC_PAGES), *([cache_v] * DEC_PAGES), k_new, v_new,
      spread, lam_vecs, subln_g)


def _split_bf16(a):
    hi = a.astype(BF16)
    lo = (a - hi.astype(F32)).astype(BF16)
    return hi, lo


def _router_kernel(x_ref, sh_ref, sc_ref, w_ref, b_ref, *rest, n_real):
    h_ref, r_ref = rest[-2:]

    @pl.when(pl.program_id(0) >= n_real)
    def _():
        h_ref[...] = jnp.zeros_like(h_ref)
        r_ref[...] = jnp.zeros_like(r_ref)

    @pl.when(pl.program_id(0) < n_real)
    def _():
        _route_rows(x_ref, sh_ref, sc_ref, w_ref, b_ref, h_ref, r_ref)


def _route_rows(x_ref, sh_ref, sc_ref, w_ref, b_ref, h_ref, r_ref):
    h = x_ref[...] * (1.0 + sc_ref[0]) + sh_ref[0]
    h_ref[...] = h
    hh, hl = _split_bf16(h)
    wh, wl = _split_bf16(w_ref[...])
    logits = _dot(hh, wh) + _dot(hl, wh) + _dot(hh, wl) + b_ref[...]

    lane = lax.broadcasted_iota(jnp.int32, logits.shape, 1).astype(F32)

    def first_max(vals):
        vmax = jnp.max(vals, axis=1, keepdims=True)
        idx = jnp.min(jnp.where(vals == vmax, lane, float(ROUTE_LANES)), axis=1, keepdims=True)
        return vmax, idx

    gl = jnp.where(lane < N_GROUPS, logits, NEG_INF)
    gmax, gsel = first_max(gl)
    pg_sel = 1.0 / jnp.sum(jnp.exp(gl - gmax), axis=1, keepdims=True)

    lo = N_GROUPS + EXPERTS_PER_GROUP * gsel
    el = jnp.where((lane >= lo) & (lane < lo + EXPERTS_PER_GROUP), logits, NEG_INF)
    v0, i0 = first_max(el)
    v1, i1 = first_max(jnp.where(lane == i0, NEG_INF, el))
    e1 = jnp.exp(v1 - v0)
    w0 = pg_sel / (1.0 + e1)
    w1 = pg_sel * e1 / (1.0 + e1)

    out = jnp.where(lane == 0, i0 - N_GROUPS, 0.0)
    out = jnp.where(lane == 1, i1 - N_GROUPS, out)
    out = jnp.where(lane == 2, w0, out)
    out = jnp.where(lane == 3, w1, out)
    r_ref[...] = out


def moe_router(seg, rows_total, row_offset, x, shift, scale, w_route, b_route, prev=None):
    assert row_offset % seg.tm == 0
    blk0 = row_offset // seg.tm
    n_real = seg.nt
    n_fill = 0
    if prev is None and rows_total > row_offset + seg.rows:
        assert row_offset == 0 and rows_total - seg.rows <= seg.tm
        n_fill = 1
    real = lambda i: jnp.minimum(i, n_real - 1)
    row = lambda i: (0, 0)
    in_specs = [
        pl.BlockSpec((seg.tm, D_MODEL), lambda i: (real(i), 0)),
        pl.BlockSpec(seg.mod_block(), lambda i: seg.mod_index(real(i))),
        pl.BlockSpec(seg.mod_block(), lambda i: seg.mod_index(real(i))),
        pl.BlockSpec((D_MODEL, ROUTE_LANES), row),
        pl.BlockSpec((1, ROUTE_LANES), row),
    ]
    args = [x, shift, scale, w_route, b_route]
    aliases = {}
    if prev is not None:
        in_specs += [pl.BlockSpec(memory_space=pl.ANY), pl.BlockSpec(memory_space=pl.ANY)]
        aliases = {len(args): 0, len(args) + 1: 1}
        args += list(prev)
    return pl.pallas_call(
        functools.partial(_router_kernel, n_real=n_real),
        out_shape=(jax.ShapeDtypeStruct((rows_total, D_MODEL), F32),
                   jax.ShapeDtypeStruct((rows_total, ROUTE_LANES), F32)),
        grid=(n_real + n_fill,),
        in_specs=in_specs,
        out_specs=(pl.BlockSpec((seg.tm, D_MODEL), lambda i: (blk0 + i, 0)),
                   pl.BlockSpec((seg.tm, ROUTE_LANES), lambda i: (blk0 + i, 0))),
        input_output_aliases=aliases,
        compiler_params=_params("arbitrary"),
        name="moe_router",
    )(*args)


def route_layout(expert_ids, n_tiles):
    n_assign = expert_ids.size
    e = expert_ids.reshape(n_assign)
    onehot = (e[:, None] == jnp.arange(N_EXPERTS, dtype=jnp.int32)[None, :]).astype(jnp.int32)
    csum = jnp.cumsum(onehot, axis=0)
    rank = jnp.sum(csum * onehot, axis=1) - 1
    counts = csum[-1]
    padded = ((counts + MOE_TM - 1) // MOE_TM) * MOE_TM
    ends = jnp.cumsum(padded)
    starts = ends - padded
    pos = starts[e] + rank
    src_rows = jnp.zeros(((n_tiles + 1) * MOE_TM,), jnp.int32).at[pos].set(
        jnp.arange(n_assign, dtype=jnp.int32) // TOP_K)
    tile_start = jnp.arange(n_tiles, dtype=jnp.int32) * MOE_TM
    valid = tile_start < ends[-1]
    tile_e = jnp.minimum(jnp.searchsorted(ends, tile_start, side="right"), N_EXPERTS - 1).astype(jnp.int32)
    last_e = jnp.max(jnp.where(valid, tile_e, 0))
    tile_e = jnp.where(valid, tile_e, last_e)
    prev_e = jnp.concatenate([jnp.full((1,), -1, jnp.int32), tile_e[:-1]])
    first = valid & (tile_e != prev_e)
    return pos.astype(jnp.int32), src_rows, tile_e, valid.astype(jnp.int32), first.astype(jnp.int32)


def _moe_kernel(te_ref, tv_ref, tf_ref, src_ref, h_hbm, win_ref, wout_ref, y_ref, buf, sem, win_b, wout_b, *, n_tiles):
    i = pl.program_id(0)
    slot = i % 2

    def row_copy(tile, r, dst_slot):
        src = src_ref[tile * MOE_TM + r]
        return pltpu.make_async_copy(h_hbm.at[pl.ds(src, 1)], buf.at[dst_slot, pl.ds(r, 1)], sem.at[dst_slot])

    def wait_tile(dst_slot):
        pltpu.make_async_copy(h_hbm.at[pl.ds(0, MOE_TM)], buf.at[dst_slot], sem.at[dst_slot]).wait()

    @pl.when(i == 0)
    def _():
        def body(r, carry):
            row_copy(0, r, 0).start()
            return carry
        lax.fori_loop(0, MOE_TM, body, 0)

    @pl.when(tf_ref[i] == 1)
    def _():
        win_b[...] = win_ref[0, 0].astype(BF16)
        wout_b[...] = wout_ref[0, 0].astype(BF16)

    @pl.when(tv_ref[i] == 1)
    def _():
        wait_tile(slot)
        x = buf[slot].astype(BF16)
        for r in range(MOE_TM):
            row_copy(i + 1, r, 1 - slot).start()
        u = _dot(x, win_b[...])
        a = u[:, :D_EXPERT]
        act = a * jax.nn.sigmoid(a) * u[:, D_EXPERT:]
        y_ref[...] = _dot(act.astype(BF16), wout_b[...])

        @pl.when(i == n_tiles - 1)
        def _():
            wait_tile(1 - slot)

    @pl.when(tv_ref[i] == 0)
    def _():
        @pl.when(tv_ref[jnp.maximum(i - 1, 0)] == 1)
        def _():
            wait_tile(slot)
        y_ref[...] = jnp.zeros_like(y_ref)


def moe_experts(layer, tile_e, tile_valid, tile_first, src_rows, h_all, w_in, w_out):
    n_tiles = tile_e.shape[0]
    kern = functools.partial(_moe_kernel, n_tiles=n_tiles)
    grid_spec = pltpu.PrefetchScalarGridSpec(
        num_scalar_prefetch=4,
        grid=(n_tiles,),
        in_specs=[
            pl.BlockSpec(memory_space=pl.ANY),
            pl.BlockSpec((1, 1, D_MODEL, 2 * D_EXPERT), lambda i, te, tv, tf, sr: (layer, te[i], 0, 0)),
            pl.BlockSpec((1, 1, D_EXPERT, D_MODEL), lambda i, te, tv, tf, sr: (layer, te[i], 0, 0)),
        ],
        out_specs=pl.BlockSpec((MOE_TM, D_MODEL), lambda i, te, tv, tf, sr: (i, 0)),
        scratch_shapes=[
            pltpu.VMEM((2, MOE_TM, D_MODEL), F32),
            pltpu.SemaphoreType.DMA((2,)),
            pltpu.VMEM((D_MODEL, 2 * D_EXPERT), BF16),
            pltpu.VMEM((D_EXPERT, D_MODEL), BF16),
        ],
    )
    return pl.pallas_call(
        kern,
        out_shape=jax.ShapeDtypeStruct((n_tiles * MOE_TM, D_MODEL), F32),
        grid_spec=grid_spec,
        compiler_params=_params("arbitrary"),
        name="moe_experts",
    )(tile_e, tile_valid, tile_first, src_rows, h_all, w_in, w_out)


def _combine_kernel(pos_ref, y_hbm, r_ref, x_ref, gate_ref, g_ref, b_ref, o_ref, buf, sem, *, tm, nt):
    i = pl.program_id(0)
    slot = i % 2

    def row_copy(tile, r, k, dst_slot):
        src = pos_ref[(tile * tm + r) * TOP_K + k]
        return pltpu.make_async_copy(y_hbm.at[pl.ds(src, 1)], buf.at[dst_slot, k, pl.ds(r, 1)], sem.at[dst_slot])

    def wait_tile(dst_slot):
        for k in range(TOP_K):
            pltpu.make_async_copy(y_hbm.at[pl.ds(0, tm)], buf.at[dst_slot, k], sem.at[dst_slot]).wait()

    @pl.when(i == 0)
    def _():
        def body(r, carry):
            for k in range(TOP_K):
                row_copy(0, r, k, 0).start()
            return carry
        lax.fori_loop(0, tm, body, 0)

    wait_tile(slot)
    y0 = buf[slot, 0]
    y1 = buf[slot, 1]
    for r in range(tm):
        for k in range(TOP_K):
            row_copy(i + 1, r, k, 1 - slot).start()
    route = r_ref[...]
    moe = route[:, 2:3] * y0 + route[:, 3:4] * y1
    z = ALPHA * x_ref[...] + gate_ref[0] * moe
    o_ref[...] = _layer_norm(z, g_ref[...], b_ref[...])

    @pl.when(i == nt - 1)
    def _():
        wait_tile(1 - slot)


def moe_combine_postnorm(seg, pos, y_sorted, route, x, gate, ln_g, ln_b):
    kern = functools.partial(_combine_kernel, tm=seg.tm, nt=seg.nt)
    pos = jnp.concatenate([pos, jnp.zeros((seg.tm * TOP_K,), jnp.int32)])
    row = lambda i, ps: (0, 0)
    grid_spec = pltpu.PrefetchScalarGridSpec(
        num_scalar_prefetch=1,
        grid=(seg.nt,),
        in_specs=[
            pl.BlockSpec(memory_space=pl.ANY),
            pl.BlockSpec((seg.tm, ROUTE_LANES), lambda i, ps: (i, 0)),
            pl.BlockSpec((seg.tm, D_MODEL), lambda i, ps: (i, 0)),
            pl.BlockSpec(seg.mod_block(), lambda i, ps: seg.mod_index(i)),
            pl.BlockSpec((1, D_MODEL), row),
            pl.BlockSpec((1, D_MODEL), row),
        ],
        out_specs=pl.BlockSpec((seg.tm, D_MODEL), lambda i, ps: (i, 0)),
        scratch_shapes=[pltpu.VMEM((2, TOP_K, seg.tm, D_MODEL), F32), pltpu.SemaphoreType.DMA((2,))],
    )
    return pl.pallas_call(
        kern,
        out_shape=jax.ShapeDtypeStruct((seg.rows, D_MODEL), F32),
        grid_spec=grid_spec,
        compiler_params=_params("arbitrary"),
        name="moe_combine_postnorm",
    )(pos, y_sorted, route, x, gate, ln_g, ln_b)


def _rope_tables(pos):
    half = HEAD_DIM // 2
    inv = ROPE_THETA ** (-jnp.arange(half, dtype=F32) / half)
    ang = pos.astype(F32)[:, None] * inv[None, :]
    cos = jnp.cos(ang)
    sin = jnp.sin(ang)
    return jnp.concatenate([cos, cos], axis=-1), jnp.concatenate([-sin, sin], axis=-1)


def kernel(x_prompt, x_sample, cache_k, cache_v, state_conv, page_table, c_prompt, c_sample, ada_w, ada_b, ln_g, ln_b, conv_w_in, conv_b_in, conv_dw, conv_dw_b, conv_ln_g, conv_ln_b, conv_w_out, conv_b_out, attn_w_qkv, attn_lambda, attn_subln_g, attn_w_out, moe_w_group, moe_b_group, moe_w_expert, moe_b_expert, moe_w_in, moe_w_out):
    bp, tp, d = x_prompt.shape
    bs, ts, _ = x_sample.shape
    n_pages = page_table.shape[1]
    past_len = n_pages * PAGE_SIZE
    mp, ms = bp * tp, bs * ts
    assert d == D_MODEL and tp % 512 == 0 and ms % 8 == 0

    seg_p = Seg(mp, 512, tp, per_row=False)
    seg_p256 = Seg(mp, 256, tp, per_row=False)
    seg_s = Seg(ms, ms, ts, per_row=True)

    n_c = bp + bs
    c_rows = -(-n_c // 8) * 8
    c_pad = jnp.concatenate([c_prompt, c_sample, jnp.zeros((c_rows - n_c, d), F32)], axis=0)
    mods = ada_all(c_pad, ada_w.reshape(DEPTH * 2, d, 3 * d), ada_b.reshape(DEPTH * 2, 1, 3 * d))

    def mod_params(i, sub):
        m = mods[i * 2 + sub]
        out_p = [m[:bp, c * d:(c + 1) * d].reshape(bp, 1, d) for c in range(3)]
        out_s = [jnp.repeat(m[bp:n_c, c * d:(c + 1) * d], ts, axis=0).reshape(1, ms, d) for c in range(3)]
        return out_p, out_s

    cos_p, sin_p = _rope_tables(jnp.arange(tp, dtype=jnp.int32))
    cos_s, sin_s = _rope_tables(jnp.tile(past_len + jnp.arange(ts, dtype=jnp.int32), bs))

    x_p = x_prompt.reshape(mp, d)
    x_s = x_sample.reshape(ms, d)
    n_attn, pool = cache_k.shape[:2]
    n_hj = 2 * N_HEADS
    cache_k2 = cache_k.reshape(n_attn, pool, PAGE_SIZE * n_hj, HEAD_DIM)
    cache_v2 = cache_v.reshape(n_attn, pool, PAGE_SIZE * N_HEADS, V_DIM)

    n_assign = (mp + ms) * TOP_K
    n_tiles = -(-n_assign // MOE_TM) + N_EXPERTS - 1

    kv_p = kv_s = None
    conv_p_rows, conv_s_rows = [], []
    row2 = lambda a: a.reshape(1, -1)

    for i in range(DEPTH):
        l = i // N_MIXERS
        (sh_p, sc_p, gt_p), (sh_s, sc_s, gt_s) = mod_params(i, 0)
        lg, lb = row2(ln_g[i, 0]), row2(ln_b[i, 0])
        if i % N_MIXERS == 0:
            w_in = conv_w_in[l].astype(BF16)
            b_in = row2(conv_b_in[l])
            w_out = conv_w_out[l].astype(BF16)
            b_out = row2(conv_b_out[l])
            dw = jnp.concatenate([conv_dw[l], jnp.zeros((HALO - CONV_WIDTH, d), F32)], axis=0)
            cw = (dw, row2(conv_dw_b[l]), row2(conv_ln_g[l]), row2(conv_ln_b[l]))

            glu_p = mod_mm_glu(seg_p, x_p, sh_p, sc_p, w_in, b_in).reshape(bp, tp, d)
            tt = 256
            halo_idx = lambda b, t: (b, jnp.maximum(t * (tt // HALO) - 1, 0), 0)
            y_p = dwconv_ln_silu(glu_p, glu_p, halo_idx, True, tt, *cw).reshape(mp, d)
            conv_p_rows.append(glu_p[:, tp - (CONV_WIDTH - 1):])
            x_p = mm_postnorm(seg_p256, y_p, w_out, b_out, x_p, gt_p, lg, lb)

            glu_s = mod_mm_glu(seg_s, x_s, sh_s, sc_s, w_in, b_in).reshape(bs, ts, d)
            hist = state_conv[l].astype(F32)
            halo_s = jnp.concatenate([jnp.zeros((bs, HALO - (CONV_WIDTH - 1), d), F32), hist], axis=1)
            cur_s = jnp.concatenate([glu_s, jnp.zeros((bs, 8 - ts, d), F32)], axis=1)
            y_s = dwconv_ln_silu(cur_s, halo_s, lambda b, t: (b, 0, 0), False, 8, *cw)[:, :ts].reshape(ms, d)
            conv_s_rows.append(jnp.concatenate([hist, glu_s], axis=1)[:, ts:])
            x_s = mm_postnorm(seg_s, y_s, w_out, b_out, x_s, gt_s, lg, lb)
        else:
            lam_init = 0.8 - 0.6 * math.exp(-0.3 * i)
            w_qkv = attn_w_qkv[l].astype(BF16)
            w_out = attn_w_out[l].astype(BF16)
            no_bias = jnp.zeros((1, d), F32)
            lam_vecs = attn_lambda[l].astype(F32)
            sub_g = row2(attn_subln_g[l])

            kv_p = mod_mm_qkv(seg_p, l, n_attn, x_p, sh_p, sc_p, w_qkv, cos_p, sin_p, prev=kv_p)
            qkv_p_bf = kv_p[2]
            kv_p = kv_p[:2]
            o_p = flash_diff_attention(qkv_p_bf, bp, tp, lam_vecs, sub_g, lam_init)
            x_p = mm_postnorm(seg_p256, o_p, w_out, no_bias, x_p, gt_p, lg, lb)

            kv_s = mod_mm_qkv(seg_s, l, n_attn, x_s, sh_s, sc_s, w_qkv, cos_s, sin_s, prev=kv_s)
            qkv_s_bf = kv_s[2]
            kv_s = kv_s[:2]
            q_rows = qkv_s_bf[0].astype(F32).reshape(bs, ts, n_hj, HEAD_DIM).transpose(0, 2, 1, 3)
            q_rows = jnp.pad(q_rows, ((0, 0), (0, 0), (0, Q_ROWS - ts), (0, 0))).reshape(bs, n_hj * Q_ROWS, HEAD_DIM)
            k_new = jnp.pad(kv_s[0][l].reshape(bs, ts * n_hj, HEAD_DIM), ((0, 0), (0, (PAGE_SIZE - ts) * n_hj), (0, 0)))
            v_new = jnp.pad(kv_s[1][l].reshape(bs, ts * N_HEADS, V_DIM), ((0, 0), (0, (PAGE_SIZE - ts) * N_HEADS), (0, 0)))
            o_s = paged_diff_attention(l, page_table, q_rows, cache_k2, cache_v2, k_new, v_new, lam_vecs, sub_g, lam_init)
            o_s = o_s.reshape(bs, N_HEADS, Q_ROWS, V_DIM)[:, :, :ts].transpose(0, 2, 1, 3).reshape(ms, d)
            x_s = mm_postnorm(seg_s, o_s, w_out, no_bias, x_s, gt_s, lg, lb)

        (sh_p, sc_p, gt_p), (sh_s, sc_s, gt_s) = mod_params(i, 1)
        lg, lb = row2(ln_g[i, 1]), row2(ln_b[i, 1])
        n_logit = N_GROUPS + N_EXPERTS
        w_route = jnp.concatenate([moe_w_group[i], moe_w_expert[i], jnp.zeros((d, ROUTE_LANES - n_logit), F32)], axis=1)
        b_route = jnp.concatenate([moe_b_group[i], moe_b_expert[i], jnp.zeros((ROUTE_LANES - n_logit,), F32)]).reshape(1, -1)
        routed = moe_router(seg_p, mp + ms, 0, x_p, sh_p, sc_p, w_route, b_route)
        h_all, route = moe_router(seg_s, mp + ms, mp, x_s, sh_s, sc_s, w_route, b_route, prev=routed)
        ids = route[:, :TOP_K].astype(jnp.int32)
        pos, src_rows, tile_e, tile_valid, tile_first = route_layout(ids, n_tiles)
        y_sorted = moe_experts(i, tile_e, tile_valid, tile_first, src_rows, h_all, moe_w_in, moe_w_out)
        x_p = moe_combine_postnorm(seg_p256, pos[:mp * TOP_K], y_sorted, route[:mp], x_p, gt_p, lg, lb)
        x_s = moe_combine_postnorm(seg_s, pos[mp * TOP_K:], y_sorted, route[mp:], x_s, gt_s, lg, lb)

    return (x_p.reshape(bp, tp, d), x_s.reshape(bs, ts, d),
            kv_p[0].reshape(n_attn, bp, tp, N_HEADS, 2, HEAD_DIM), kv_p[1].reshape(n_attn, bp, tp, N_HEADS, V_DIM),
            jnp.stack(conv_p_rows),
            kv_s[0].reshape(n_attn, bs, ts, N_HEADS, 2, HEAD_DIM), kv_s[1].reshape(n_attn, bs, ts, N_HEADS, V_DIM),
            jnp.stack(conv_s_rows))
```

```python
import functools
import math

import jax
import jax.numpy as jnp
from jax import lax
from jax.experimental import pallas as pl
from jax.experimental.pallas import tpu as pltpu

D_MODEL = 2048
DEPTH = 4
PAGE_SIZE = 128
N_MIXERS = 2
CONV_WIDTH = 31
N_HEADS = 8
HEAD_DIM = D_MODEL // (2 * N_HEADS)
V_DIM = 2 * HEAD_DIM
ROPE_THETA = 10000.0
N_GROUPS = 4
EXPERTS_PER_GROUP = 8
N_EXPERTS = N_GROUPS * EXPERTS_PER_GROUP
TOP_K = 2
D_EXPERT = D_MODEL // 4
ALPHA = (2 * DEPTH) ** 0.25
LN_EPS = 1e-5

F32 = jnp.float32
BF16 = jnp.bfloat16
LANES = 128
HALO = 32
VMEM_LIMIT = 56 * 1024 * 1024
MOE_TM = 256
ROUTE_LANES = 128
NEG_INF = float("-inf")
Q_SCALE = HEAD_DIM ** -0.5 * math.log2(math.e)
DEC_PAGES = 8
Q_ROWS = 8
GATHER_AHEAD = 2


def _params(*sem):
    return pltpu.CompilerParams(dimension_semantics=sem, vmem_limit_bytes=VMEM_LIMIT)


def _dot(a, b):
    return jnp.dot(a, b, preferred_element_type=F32)


def _dot_nt(a, b):
    return lax.dot_general(a, b, (((1,), (1,)), ((), ())), preferred_element_type=F32)


def _layer_norm(z, g, b):
    mu = jnp.mean(z, axis=-1, keepdims=True)
    zc = z - mu
    var = jnp.mean(zc * zc, axis=-1, keepdims=True)
    return zc * lax.rsqrt(var + LN_EPS) * g + b


class Seg:
    def __init__(self, rows, tm, rows_per_batch, per_row):
        self.rows = rows
        self.tm = tm
        self.nt = rows // tm
        self.per_row = per_row
        self.tiles_per_batch = max(rows_per_batch // tm, 1)

    def mod_block(self):
        return (1, self.tm if self.per_row else 1, D_MODEL)

    def mod_index(self, i):
        return (0 if self.per_row else i // self.tiles_per_batch, 0, 0)

    def pos_index(self, i):
        return (0 if self.per_row else i % self.tiles_per_batch, 0)


def _ada_kernel(c_ref, w_ref, b_ref, o_ref):
    o_ref[0] = _dot(c_ref[...].astype(BF16), w_ref[0].astype(BF16)) + b_ref[0]


def ada_all(c_pad, w, b):
    n_sub, _, n_out = w.shape
    rows = c_pad.shape[0]
    tn = 1024
    return pl.pallas_call(
        _ada_kernel,
        out_shape=jax.ShapeDtypeStruct((n_sub, rows, n_out), F32),
        grid=(n_sub, n_out // tn),
        in_specs=[
            pl.BlockSpec((rows, D_MODEL), lambda s, j: (0, 0)),
            pl.BlockSpec((1, D_MODEL, tn), lambda s, j: (s, 0, j)),
            pl.BlockSpec((1, 1, tn), lambda s, j: (s, 0, j)),
        ],
        out_specs=pl.BlockSpec((1, rows, tn), lambda s, j: (s, 0, j)),
        compiler_params=_params("arbitrary", "arbitrary"),
        name="ada_all",
    )(c_pad, w, b)


def _glu_kernel(x_ref, sh_ref, sc_ref, wa_ref, wg_ref, ba_ref, bg_ref, o_ref, h_scr):
    @pl.when(pl.program_id(1) == 0)
    def _():
        h_scr[...] = (x_ref[...] * (1.0 + sc_ref[0]) + sh_ref[0]).astype(BF16)

    h = h_scr[...]
    a = _dot(h, wa_ref[...]) + ba_ref[...]
    g = _dot(h, wg_ref[...]) + bg_ref[...]
    o_ref[...] = a * jax.nn.sigmoid(g)


def mod_mm_glu(seg, x, shift, scale, w_in, b_in):
    tn = 512
    nj = D_MODEL // tn
    return pl.pallas_call(
        _glu_kernel,
        out_shape=jax.ShapeDtypeStruct((seg.rows, D_MODEL), F32),
        grid=(seg.nt, nj),
        in_specs=[
            pl.BlockSpec((seg.tm, D_MODEL), lambda i, j: (i, 0)),
            pl.BlockSpec(seg.mod_block(), lambda i, j: seg.mod_index(i)),
            pl.BlockSpec(seg.mod_block(), lambda i, j: seg.mod_index(i)),
            pl.BlockSpec((D_MODEL, tn), lambda i, j: (0, j)),
            pl.BlockSpec((D_MODEL, tn), lambda i, j: (0, j + nj)),
            pl.BlockSpec((1, tn), lambda i, j: (0, j)),
            pl.BlockSpec((1, tn), lambda i, j: (0, j + nj)),
        ],
        out_specs=pl.BlockSpec((seg.tm, tn), lambda i, j: (i, j)),
        scratch_shapes=[pltpu.VMEM((seg.tm, D_MODEL), BF16)],
        compiler_params=_params("arbitrary", "arbitrary"),
        name="mod_mm_glu",
    )(x, shift, scale, w_in, w_in, b_in, b_in)


def _dwconv_kernel(cur_ref, halo_ref, dw_ref, dwb_ref, g_ref, b_ref, o_ref, xs_scr, sh_scr, y_scr, *, tt, zero_first):
    hist = halo_ref[0]
    if zero_first:
        hist = jnp.where(pl.program_id(1) == 0, jnp.zeros_like(hist), hist)
    xs_scr[0:HALO, :] = hist
    xs_scr[HALO:HALO + tt, :] = cur_ref[0]

    rch = min(tt, 32)
    cch = sh_scr.shape[2]
    span = sh_scr.shape[1]
    lead = HALO - (CONV_WIDTH - 1)

    def col_body(cc, carry):
        cols = pl.ds(pl.multiple_of(cc * cch, cch), cch)
        for r in range(1, 8):
            sh_scr[r - 1] = xs_scr[r:r + span, cols]
        for rc in range(tt // rch):
            acc = jnp.zeros((rch, cch), F32)
            for j in range(CONV_WIDTH):
                q, r = divmod(lead + j, 8)
                u0 = rc * rch + 8 * q
                win = xs_scr[u0:u0 + rch, cols] if r == 0 else sh_scr[r - 1, u0:u0 + rch, :]
                acc = acc + dw_ref[j:j + 1, cols] * win
            y_scr[rc * rch:(rc + 1) * rch, cols] = acc + dwb_ref[:, cols]
        return carry

    lax.fori_loop(0, D_MODEL // cch, col_body, 0)
    y = _layer_norm(y_scr[...], g_ref[...], b_ref[...])
    o_ref[0] = (y * jax.nn.sigmoid(y)).astype(BF16)


def dwconv_ln_silu(cur, halo, halo_index, zero_first, tt, dw, dw_b, ln_g, ln_b):
    nb, t, _ = cur.shape
    kern = functools.partial(_dwconv_kernel, tt=tt, zero_first=zero_first)
    return pl.pallas_call(
        kern,
        out_shape=jax.ShapeDtypeStruct((nb, t, D_MODEL), BF16),
        grid=(nb, t // tt),
        in_specs=[
            pl.BlockSpec((1, tt, D_MODEL), lambda b, i: (b, i, 0)),
            pl.BlockSpec((1, HALO, D_MODEL), halo_index),
            pl.BlockSpec((HALO, D_MODEL), lambda b, i: (0, 0)),
            pl.BlockSpec((1, D_MODEL), lambda b, i: (0, 0)),
            pl.BlockSpec((1, D_MODEL), lambda b, i: (0, 0)),
            pl.BlockSpec((1, D_MODEL), lambda b, i: (0, 0)),
        ],
        out_specs=pl.BlockSpec((1, tt, D_MODEL), lambda b, i: (b, i, 0)),
        scratch_shapes=[pltpu.VMEM((HALO + tt, D_MODEL), F32),
                        pltpu.VMEM((7, tt + HALO - 8, 256), F32),
                        pltpu.VMEM((tt, D_MODEL), F32)],
        compiler_params=_params("arbitrary", "arbitrary"),
        name="dwconv_ln_silu",
    )(cur, halo, dw, dw_b, ln_g, ln_b)


def _mm_postnorm_kernel(y_ref, w_ref, bias_ref, x_ref, gate_ref, g_ref, b_ref, o_ref):
    out = _dot(y_ref[...], w_ref[...]) + bias_ref[...]
    z = ALPHA * x_ref[...] + gate_ref[0] * out
    o_ref[...] = _layer_norm(z, g_ref[...], b_ref[...])


def mm_postnorm(seg, y, w, bias, x, gate, ln_g, ln_b):
    k = y.shape[1]
    row = lambda i: (0, 0)
    return pl.pallas_call(
        _mm_postnorm_kernel,
        out_shape=jax.ShapeDtypeStruct((seg.rows, D_MODEL), F32),
        grid=(seg.nt,),
        in_specs=[
            pl.BlockSpec((seg.tm, k), lambda i: (i, 0)),
            pl.BlockSpec((k, D_MODEL), row),
            pl.BlockSpec((1, D_MODEL), row),
            pl.BlockSpec((seg.tm, D_MODEL), lambda i: (i, 0)),
            pl.BlockSpec(seg.mod_block(), seg.mod_index),
            pl.BlockSpec((1, D_MODEL), row),
            pl.BlockSpec((1, D_MODEL), row),
        ],
        out_specs=pl.BlockSpec((seg.tm, D_MODEL), lambda i: (i, 0)),
        compiler_params=_params("arbitrary"),
        name="mm_postnorm",
    )(y, w, bias, x, gate, ln_g, ln_b)


def _qkv_kernel(x_ref, sh_ref, sc_ref, w_ref, cos_ref, sin_ref, *rest, tm):
    ok_ref, ov_ref, ob_ref, h_scr = rest[-4:]
    part = pl.program_id(1)
    cw = 4 * HEAD_DIM
    n_hj = D_MODEL // HEAD_DIM

    def rope_chunks(store_rot):
        h = h_scr[...]
        cos = cos_ref[...]
        sin = sin_ref[...]
        for c in range(D_MODEL // cw):
            acc = _dot(h, w_ref[:, c * cw:(c + 1) * cw])
            for g in range(cw // HEAD_DIM):
                blk = acc[:, g * HEAD_DIM:(g + 1) * HEAD_DIM]
                rot = blk * cos + pltpu.roll(blk, HEAD_DIM // 2, 1) * sin
                store_rot(c * (cw // HEAD_DIM) + g, rot)

    @pl.when(part == 0)
    def _():
        h_scr[...] = (x_ref[...] * (1.0 + sc_ref[0]) + sh_ref[0]).astype(BF16)

        def store_q(hj, rot):
            ob_ref[0, :, hj * HEAD_DIM:(hj + 1) * HEAD_DIM] = (rot * Q_SCALE).astype(BF16)
        rope_chunks(store_q)

    @pl.when(part == 1)
    def _():
        def store_k(hj, rot):
            ob_ref[0, :, hj * HEAD_DIM:(hj + 1) * HEAD_DIM] = rot.astype(BF16)
            ok_ref[0, pl.ds(hj, tm, stride=n_hj), :] = rot
        rope_chunks(store_k)

    @pl.when(part == 2)
    def _():
        h = h_scr[...]
        for c in range(D_MODEL // cw):
            acc = _dot(h, w_ref[:, c * cw:(c + 1) * cw])
            ob_ref[0, :, c * cw:(c + 1) * cw] = acc.astype(BF16)
            ov_ref[:, c * cw:(c + 1) * cw] = acc

    @pl.when(part > 2)
    def _():
        ok_ref[...] = jnp.zeros_like(ok_ref)


def mod_mm_qkv(seg, layer, n_layers, x, shift, scale, w_qkv, cos, sin, prev_k=None):
    n_hj = D_MODEL // HEAD_DIM
    aliased = prev_k is not None
    n_fill = 0 if aliased else n_layers - 1
    part = lambda p: jnp.minimum(p, 2)

    def k_slab(p):
        e = p - 3
        return jnp.where(p < 3, layer, jnp.where(e < layer, e, e + 1))

    in_specs = [
        pl.BlockSpec((seg.tm, D_MODEL), lambda i, p: (i, 0)),
        pl.BlockSpec(seg.mod_block(), lambda i, p: seg.mod_index(i)),
        pl.BlockSpec(seg.mod_block(), lambda i, p: seg.mod_index(i)),
        pl.BlockSpec((D_MODEL, D_MODEL), lambda i, p: (0, part(p))),
        pl.BlockSpec((seg.tm, HEAD_DIM), lambda i, p: seg.pos_index(i)),
        pl.BlockSpec((seg.tm, HEAD_DIM), lambda i, p: seg.pos_index(i)),
    ]
    args = [x, shift, scale, w_qkv, cos, sin]
    aliases = {}
    if aliased:
        in_specs.append(pl.BlockSpec(memory_space=pl.ANY))
        aliases = {len(args): 0}
        args.append(prev_k)
    return pl.pallas_call(
        functools.partial(_qkv_kernel, tm=seg.tm),
        out_shape=(jax.ShapeDtypeStruct((n_layers, seg.rows * n_hj, HEAD_DIM), F32),
                   jax.ShapeDtypeStruct((seg.rows, D_MODEL), F32),
                   jax.ShapeDtypeStruct((3, seg.rows, D_MODEL), BF16)),
        grid=(seg.nt, 3 + n_fill),
        in_specs=in_specs,
        out_specs=(pl.BlockSpec((1, seg.tm * n_hj, HEAD_DIM), lambda i, p: (k_slab(p), i, 0)),
                   pl.BlockSpec((seg.tm, D_MODEL), lambda i, p: (i, 0)),
                   pl.BlockSpec((1, seg.tm, D_MODEL), lambda i, p: (part(p), i, 0))),
        scratch_shapes=[pltpu.VMEM((seg.tm, D_MODEL), BF16)],
        input_output_aliases=aliases,
        compiler_params=_params("arbitrary", "arbitrary"),
        name="mod_mm_qkv",
    )(*args)


def _diff_lambda(lam_ref, lam_init):
    lv = lam_ref[...]
    s01 = jnp.sum(lv[0:1] * lv[1:2], axis=-1, keepdims=True)
    s23 = jnp.sum(lv[2:3] * lv[3:4], axis=-1, keepdims=True)
    return jnp.exp(s01) - jnp.exp(s23) + lam_init


def _head_rms(o, g, lam_init):
    ms = jnp.mean(o * o, axis=-1, keepdims=True)
    return o * lax.rsqrt(ms + LN_EPS) * g * (1.0 - lam_init)


def _flash_kernel(qi_ref, ki_ref, q_ref, k_ref, v_ref, lam_ref, g_ref, o_ref, m_scr, l_scr, acc_scr, s_scr, p_scr, a_scr,
                  *, tq, lam_init):
    pr = pl.program_id(2)
    qi = qi_ref[pr]
    ki = ki_ref[pr]
    rc = 32
    n_kt = tq // LANES

    @pl.when(ki == 0)
    def _():
        m_scr[...] = jnp.full_like(m_scr, NEG_INF)
        l_scr[...] = jnp.zeros_like(l_scr)
        acc_scr[...] = jnp.zeros_like(acc_scr)

    def step(masked):
        for j in range(2):
            lanes = slice(j * HEAD_DIM, (j + 1) * HEAD_DIM)
            s_scr[j] = _dot_nt(q_ref[0, :, lanes], k_ref[0, :, lanes])
        for j in range(2):
            def chunk(c):
                r0 = c * rc
                rows = slice(r0, r0 + rc)
                parts = [s_scr[j, rows, kt * LANES:(kt + 1) * LANES] for kt in range(n_kt)]
                if masked:
                    row = r0 + lax.broadcasted_iota(jnp.int32, (rc, LANES), 0)
                    col = lax.broadcasted_iota(jnp.int32, (rc, LANES), 1)
                    parts = [jnp.where(col + kt * LANES <= row, parts[kt], NEG_INF) for kt in range(n_kt)]
                mx = parts[0]
                for kt in range(1, n_kt):
                    mx = jnp.maximum(mx, parts[kt])
                m_old = m_scr[j, rows, :]
                m_new = jnp.maximum(m_old, jnp.max(mx, axis=1, keepdims=True))
                alpha = jnp.exp2(m_old - m_new)
                lsum = jnp.zeros((rc, LANES), F32)
                for kt in range(n_kt):
                    p = jnp.exp2(parts[kt] - m_new)
                    lsum = lsum + p
                    p_scr[j, rows, kt * LANES:(kt + 1) * LANES] = p.astype(BF16)
                l_scr[j, rows, :] = alpha * l_scr[j, rows, :] + lsum
                m_scr[j, rows, :] = m_new
                a_scr[j, rows, :] = alpha

            for c in range(tq // rc):
                chunk(c)
            alpha = a_scr[j]
            acc_scr[j] = jnp.concatenate([alpha, alpha], axis=1) * acc_scr[j] + _dot(p_scr[j], v_ref[0])

    @pl.when(ki < qi)
    def _():
        step(False)

    @pl.when(ki == qi)
    def _():
        step(True)
        lam = _diff_lambda(lam_ref, lam_init)
        l0 = jnp.sum(l_scr[0], axis=1, keepdims=True)
        l1 = jnp.sum(l_scr[1], axis=1, keepdims=True)
        o = acc_scr[0] / l0 - lam * (acc_scr[1] / l1)
        o_ref[...] = _head_rms(o, g_ref[...], lam_init).astype(BF16)


def flash_diff_attention(qkv_bf, nb, t, lam_vecs, subln_g, lam_init):
    tq = 512
    nq = t // tq
    pairs = [(a, b) for a in range(nq) for b in range(a + 1)]
    qi_tab = jnp.asarray([p[0] for p in pairs], jnp.int32)
    ki_tab = jnp.asarray([p[1] for p in pairs], jnp.int32)
    kern = functools.partial(_flash_kernel, tq=tq, lam_init=lam_init)
    grid_spec = pltpu.PrefetchScalarGridSpec(
        num_scalar_prefetch=2,
        grid=(nb, N_HEADS, len(pairs)),
        in_specs=[
            pl.BlockSpec((1, tq, V_DIM), lambda b, h, p, qt, kt: (0, b * nq + qt[p], h)),
            pl.BlockSpec((1, tq, V_DIM), lambda b, h, p, qt, kt: (1, b * nq + kt[p], h)),
            pl.BlockSpec((1, tq, V_DIM), lambda b, h, p, qt, kt: (2, b * nq + kt[p], h)),
            pl.BlockSpec((4, HEAD_DIM), lambda b, h, p, qt, kt: (0, 0)),
            pl.BlockSpec((1, V_DIM), lambda b, h, p, qt, kt: (0, 0)),
        ],
        out_specs=pl.BlockSpec((tq, V_DIM), lambda b, h, p, qt, kt: (b * nq + qt[p], h)),
        scratch_shapes=[
            pltpu.VMEM((2, tq, LANES), F32),
            pltpu.VMEM((2, tq, LANES), F32),
            pltpu.VMEM((2, tq, V_DIM), F32),
            pltpu.VMEM((2, tq, tq), F32),
            pltpu.VMEM((2, tq, tq), BF16),
            pltpu.VMEM((2, tq, LANES), F32),
        ],
    )
    return pl.pallas_call(
        kern,
        out_shape=jax.ShapeDtypeStruct((nb * t, D_MODEL), BF16),
        grid_spec=grid_spec,
        compiler_params=_params("arbitrary", "arbitrary", "arbitrary"),
        name="flash_diff_attention",
    )(qi_tab, ki_tab, qkv_bf, qkv_bf, qkv_bf, lam_vecs, subln_g)


def _decode_kernel(pt_ref, q_ref, *refs, n_steps, lam_init):
    k_refs = refs[:DEC_PAGES]
    v_refs = refs[DEC_PAGES:2 * DEC_PAGES]
    kn_ref, vn_ref, spread_ref, lam_ref, g_ref, o_ref, m_scr, l_scr, acc_scr, own_scr = refs[2 * DEC_PAGES:]
    p = pl.program_id(1)
    n_hj = 2 * N_HEADS
    grp = 2 * Q_ROWS

    @pl.when(p == 0)
    def _():
        m_scr[...] = jnp.full_like(m_scr, NEG_INF)
        l_scr[...] = jnp.zeros_like(l_scr)
        acc_scr[...] = jnp.zeros_like(acc_scr)
        row_head = lax.broadcasted_iota(jnp.int32, own_scr.shape, 0) // grp
        col_head = lax.broadcasted_iota(jnp.int32, own_scr.shape, 1) % N_HEADS
        own_scr[...] = (row_head == col_head).astype(F32)

    def process(k_pages, v_pages, keep):
        q_parts = [q_ref[0, hj * Q_ROWS:(hj + 1) * Q_ROWS, :].astype(BF16) for hj in range(n_hj)]
        s_pages = []
        for k_ref in k_pages:
            parts = []
            for hj in range(n_hj):
                k_hj = k_ref[pl.ds(hj, PAGE_SIZE, stride=n_hj), :].astype(BF16)
                parts.append(_dot_nt(q_parts[hj], k_hj))
            s_pages.append(jnp.concatenate(parts, axis=0))
        s = jnp.concatenate(s_pages, axis=1)
        if keep is not None:
            s = jnp.where(keep, s, NEG_INF)
        m_old = m_scr[...]
        m_new = jnp.maximum(m_old, jnp.max(s, axis=1, keepdims=True))
        alpha = jnp.exp2(m_old - m_new)
        pe = jnp.exp2(s - m_new)
        l_scr[...] = alpha * l_scr[...] + jnp.sum(pe, axis=1, keepdims=True)
        m_scr[...] = m_new
        own = own_scr[...]
        pv = None
        for c, v_ref in enumerate(v_pages):
            pe_c = pe[:, c * PAGE_SIZE:(c + 1) * PAGE_SIZE].astype(BF16)
            spread = _dot(pe_c, spread_ref[...]) * own
            term = _dot(spread.astype(BF16), v_ref[...].astype(BF16))
            pv = term if pv is None else pv + term
        acc_scr[...] = alpha * acc_scr[...] + pv

    process([r.at[0, 0] for r in k_refs], [r.at[0, 0] for r in v_refs], None)

    @pl.when(p == n_steps - 1)
    def _():
        row = lax.broadcasted_iota(jnp.int32, (n_hj * Q_ROWS, PAGE_SIZE), 0)
        col = lax.broadcasted_iota(jnp.int32, (n_hj * Q_ROWS, PAGE_SIZE), 1)
        process([kn_ref.at[0]], [vn_ref.at[0]], col <= row % Q_ROWS)

        lam = _diff_lambda(lam_ref, lam_init)
        an = acc_scr[...] / l_scr[...]
        outs = []
        for h in range(N_HEADS):
            outs.append(an[h * grp:h * grp + Q_ROWS] - lam * an[h * grp + Q_ROWS:(h + 1) * grp])
        o = jnp.concatenate(outs, axis=0)
        o_ref[0] = _head_rms(o, g_ref[...], lam_init).astype(BF16)


def paged_diff_attention(layer, page_table, q_rows, cache_k, cache_v, k_new, v_new, lam_vecs, subln_g, lam_init):
    nb, n_pages = page_table.shape
    assert n_pages % DEC_PAGES == 0
    n_steps = n_pages // DEC_PAGES
    n_hj = 2 * N_HEADS
    kern = functools.partial(_decode_kernel, n_steps=n_steps, lam_init=lam_init)

    def page_idx(c):
        return lambda b, p, pt: (layer, pt[b * n_pages + p * DEC_PAGES + c], 0, 0)

    fixed = lambda b, p, pt: (b, 0, 0)
    grid_spec = pltpu.PrefetchScalarGridSpec(
        num_scalar_prefetch=1,
        grid=(nb, n_steps),
        in_specs=(
            [pl.BlockSpec((1, n_hj * Q_ROWS, HEAD_DIM), fixed)]
            + [pl.BlockSpec((1, 1, PAGE_SIZE * n_hj, HEAD_DIM), page_idx(c)) for c in range(DEC_PAGES)]
            + [pl.BlockSpec((1, 1, PAGE_SIZE * N_HEADS, V_DIM), page_idx(c)) for c in range(DEC_PAGES)]
            + [pl.BlockSpec((1, PAGE_SIZE * n_hj, HEAD_DIM), fixed),
               pl.BlockSpec((1, PAGE_SIZE * N_HEADS, V_DIM), fixed),
               pl.BlockSpec((PAGE_SIZE, PAGE_SIZE * N_HEADS), lambda b, p, pt: (0, 0)),
               pl.BlockSpec((4, HEAD_DIM), lambda b, p, pt: (0, 0)),
               pl.BlockSpec((1, V_DIM), lambda b, p, pt: (0, 0))]),
        out_specs=pl.BlockSpec((1, N_HEADS * Q_ROWS, V_DIM), fixed),
        scratch_shapes=[
            pltpu.VMEM((n_hj * Q_ROWS, 1), F32),
            pltpu.VMEM((n_hj * Q_ROWS, 1), F32),
            pltpu.VMEM((n_hj * Q_ROWS, V_DIM), F32),
            pltpu.VMEM((n_hj * Q_ROWS, PAGE_SIZE * N_HEADS), F32),
        ],
    )
    spread = jnp.repeat(jnp.eye(PAGE_SIZE, dtype=BF16), N_HEADS, axis=1)
    return pl.pallas_call(
        kern,
        out_shape=jax.ShapeDtypeStruct((nb, N_HEADS * Q_ROWS, V_DIM), BF16),
        grid_spec=grid_spec,
        compiler_params=_params("arbitrary", "arbitrary"),
        name="paged_diff_attention",
    )(page_table.reshape(-1), q_rows, *([cache_k] * DEC_PAGES), *([cache_v] * DEC_PAGES), k_new, v_new,
      spread, lam_vecs, subln_g)


def _router_kernel(x_ref, sh_ref, sc_ref, w_ref, b_ref, *rest, n_real):
    h_ref, r_ref = rest[-2:]

    @pl.when(pl.program_id(0) >= n_real)
    def _():
        h_ref[...] = jnp.zeros_like(h_ref)
        r_ref[...] = jnp.zeros_like(r_ref)

    @pl.when(pl.program_id(0) < n_real)
    def _():
        _route_rows(x_ref, sh_ref, sc_ref, w_ref, b_ref, h_ref, r_ref)


def _route_rows(x_ref, sh_ref, sc_ref, w_ref, b_ref, h_ref, r_ref):
    h = x_ref[...] * (1.0 + sc_ref[0]) + sh_ref[0]
    h_ref[...] = h
    logits = _dot(h.astype(BF16), w_ref[...].astype(BF16)) + b_ref[...]

    lane = lax.broadcasted_iota(jnp.int32, logits.shape, 1).astype(F32)

    def first_max(vals):
        vmax = jnp.max(vals, axis=1, keepdims=True)
        idx = jnp.min(jnp.where(vals == vmax, lane, float(ROUTE_LANES)), axis=1, keepdims=True)
        return vmax, idx

    gl = jnp.where(lane < N_GROUPS, logits, NEG_INF)
    gmax, gsel = first_max(gl)
    pg_sel = 1.0 / jnp.sum(jnp.exp(gl - gmax), axis=1, keepdims=True)

    lo = N_GROUPS + EXPERTS_PER_GROUP * gsel
    el = jnp.where((lane >= lo) & (lane < lo + EXPERTS_PER_GROUP), logits, NEG_INF)
    v0, i0 = first_max(el)
    v1, i1 = first_max(jnp.where(lane == i0, NEG_INF, el))
    e1 = jnp.exp(v1 - v0)
    w0 = pg_sel / (1.0 + e1)
    w1 = pg_sel * e1 / (1.0 + e1)

    out = jnp.where(lane == 0, i0 - N_GROUPS, 0.0)
    out = jnp.where(lane == 1, i1 - N_GROUPS, out)
    out = jnp.where(lane == 2, w0, out)
    out = jnp.where(lane == 3, w1, out)
    r_ref[...] = out


def moe_router(seg, rows_total, row_offset, x, shift, scale, w_route, b_route, prev=None):
    assert row_offset % seg.tm == 0
    blk0 = row_offset // seg.tm
    n_real = seg.nt
    n_fill = 0
    if prev is None and rows_total > row_offset + seg.rows:
        assert row_offset == 0 and rows_total - seg.rows <= seg.tm
        n_fill = 1
    real = lambda i: jnp.minimum(i, n_real - 1)
    row = lambda i: (0, 0)
    in_specs = [
        pl.BlockSpec((seg.tm, D_MODEL), lambda i: (real(i), 0)),
        pl.BlockSpec(seg.mod_block(), lambda i: seg.mod_index(real(i))),
        pl.BlockSpec(seg.mod_block(), lambda i: seg.mod_index(real(i))),
        pl.BlockSpec((D_MODEL, ROUTE_LANES), row),
        pl.BlockSpec((1, ROUTE_LANES), row),
    ]
    args = [x, shift, scale, w_route, b_route]
    aliases = {}
    if prev is not None:
        in_specs += [pl.BlockSpec(memory_space=pl.ANY), pl.BlockSpec(memory_space=pl.ANY)]
        aliases = {len(args): 0, len(args) + 1: 1}
        args += list(prev)
    return pl.pallas_call(
        functools.partial(_router_kernel, n_real=n_real),
        out_shape=(jax.ShapeDtypeStruct((rows_total, D_MODEL), F32),
                   jax.ShapeDtypeStruct((rows_total, ROUTE_LANES), F32)),
        grid=(n_real + n_fill,),
        in_specs=in_specs,
        out_specs=(pl.BlockSpec((seg.tm, D_MODEL), lambda i: (blk0 + i, 0)),
                   pl.BlockSpec((seg.tm, ROUTE_LANES), lambda i: (blk0 + i, 0))),
        input_output_aliases=aliases,
        compiler_params=_params("arbitrary"),
        name="moe_router",
    )(*args)


def route_layout(expert_ids, n_tiles):
    n_assign = expert_ids.size
    e = expert_ids.reshape(n_assign)
    onehot = (e[:, None] == jnp.arange(N_EXPERTS, dtype=jnp.int32)[None, :]).astype(jnp.int32)
    csum = jnp.cumsum(onehot, axis=0)
    rank = jnp.sum(csum * onehot, axis=1) - 1
    counts = csum[-1]
    padded = ((counts + MOE_TM - 1) // MOE_TM) * MOE_TM
    ends = jnp.cumsum(padded)
    starts = ends - padded
    pos = starts[e] + rank
    src_rows = jnp.zeros(((n_tiles + GATHER_AHEAD) * MOE_TM,), jnp.int32).at[pos].set(
        jnp.arange(n_assign, dtype=jnp.int32) // TOP_K)
    tile_start = jnp.arange(n_tiles, dtype=jnp.int32) * MOE_TM
    valid = tile_start < ends[-1]
    tile_e = jnp.minimum(jnp.searchsorted(ends, tile_start, side="right"), N_EXPERTS - 1).astype(jnp.int32)
    last_e = jnp.max(jnp.where(valid, tile_e, 0))
    tile_e = jnp.where(valid, tile_e, last_e)
    prev_e = jnp.concatenate([jnp.full((1,), -1, jnp.int32), tile_e[:-1]])
    first = valid & (tile_e != prev_e)
    return pos.astype(jnp.int32), src_rows, tile_e, valid.astype(jnp.int32), first.astype(jnp.int32)


def _moe_kernel(te_ref, tv_ref, tf_ref, src_ref, h_hbm, win_ref, wout_ref, y_ref, buf, sem, win_b, wout_b, *, n_tiles):
    i = pl.program_id(0)
    n_slots = buf.shape[0]
    slot = lax.rem(i, n_slots)

    def row_copy(tile, r, dst_slot):
        src = src_ref[tile * MOE_TM + r]
        return pltpu.make_async_copy(h_hbm.at[pl.ds(src, 1)], buf.at[dst_slot, pl.ds(r, 1)], sem.at[dst_slot])

    def wait_tile(dst_slot):
        pltpu.make_async_copy(h_hbm.at[pl.ds(0, MOE_TM)], buf.at[dst_slot], sem.at[dst_slot]).wait()

    def started_by_valid(tile):
        return tv_ref[jnp.maximum(tile - GATHER_AHEAD, 0)] == 1

    @pl.when(i == 0)
    def _():
        for t in range(GATHER_AHEAD):
            def body(r, carry):
                row_copy(t, r, t).start()
                return carry
            lax.fori_loop(0, MOE_TM, body, 0)

    @pl.when(tf_ref[i] == 1)
    def _():
        win_b[...] = win_ref[0, 0].astype(BF16)
        wout_b[...] = wout_ref[0, 0].astype(BF16)

    @pl.when((i < GATHER_AHEAD) | started_by_valid(i))
    def _():
        wait_tile(slot)

    @pl.when(tv_ref[i] == 1)
    def _():
        x = buf[slot].astype(BF16)
        nxt = lax.rem(i + GATHER_AHEAD, n_slots)
        for r in range(MOE_TM):
            row_copy(i + GATHER_AHEAD, r, nxt).start()
        u = _dot(x, win_b[...])
        a = u[:, :D_EXPERT]
        act = a * jax.nn.sigmoid(a) * u[:, D_EXPERT:]
        y_ref[...] = _dot(act.astype(BF16), wout_b[...])

    @pl.when(tv_ref[i] == 0)
    def _():
        y_ref[...] = jnp.zeros_like(y_ref)

    @pl.when(i == n_tiles - 1)
    def _():
        for t in range(n_tiles, n_tiles + GATHER_AHEAD):
            @pl.when(started_by_valid(t))
            def _():
                wait_tile(t % n_slots)


def moe_experts(layer, tile_e, tile_valid, tile_first, src_rows, h_all, w_in, w_out):
    n_tiles = tile_e.shape[0]
    kern = functools.partial(_moe_kernel, n_tiles=n_tiles)
    grid_spec = pltpu.PrefetchScalarGridSpec(
        num_scalar_prefetch=4,
        grid=(n_tiles,),
        in_specs=[
            pl.BlockSpec(memory_space=pl.ANY),
            pl.BlockSpec((1, 1, D_MODEL, 2 * D_EXPERT), lambda i, te, tv, tf, sr: (layer, te[i], 0, 0)),
            pl.BlockSpec((1, 1, D_EXPERT, D_MODEL), lambda i, te, tv, tf, sr: (layer, te[i], 0, 0)),
        ],
        out_specs=pl.BlockSpec((MOE_TM, D_MODEL), lambda i, te, tv, tf, sr: (i, 0)),
        scratch_shapes=[
            pltpu.VMEM((GATHER_AHEAD + 1, MOE_TM, D_MODEL), F32),
            pltpu.SemaphoreType.DMA((GATHER_AHEAD + 1,)),
            pltpu.VMEM((D_MODEL, 2 * D_EXPERT), BF16),
            pltpu.VMEM((D_EXPERT, D_MODEL), BF16),
        ],
    )
    return pl.pallas_call(
        kern,
        out_shape=jax.ShapeDtypeStruct((n_tiles * MOE_TM, D_MODEL), F32),
        grid_spec=grid_spec,
        compiler_params=_params("arbitrary"),
        name="moe_experts",
    )(tile_e, tile_valid, tile_first, src_rows, h_all, w_in, w_out)


def _combine_kernel(pos_ref, y_hbm, r_ref, x_ref, gate_ref, g_ref, b_ref, o_ref, buf, sem, *, tm, nt):
    i = pl.program_id(0)
    n_slots = buf.shape[0]
    slot = lax.rem(i, n_slots)

    def row_copy(tile, r, k, dst_slot):
        src = pos_ref[(tile * tm + r) * TOP_K + k]
        return pltpu.make_async_copy(y_hbm.at[pl.ds(src, 1)], buf.at[dst_slot, k, pl.ds(r, 1)], sem.at[dst_slot])

    def wait_tile(dst_slot):
        for k in range(TOP_K):
            pltpu.make_async_copy(y_hbm.at[pl.ds(0, tm)], buf.at[dst_slot, k], sem.at[dst_slot]).wait()

    @pl.when(i == 0)
    def _():
        for t in range(GATHER_AHEAD):
            def body(r, carry):
                for k in range(TOP_K):
                    row_copy(t, r, k, t).start()
                return carry
            lax.fori_loop(0, tm, body, 0)

    wait_tile(slot)
    y0 = buf[slot, 0]
    y1 = buf[slot, 1]
    nxt = lax.rem(i + GATHER_AHEAD, n_slots)
    for r in range(tm):
        for k in range(TOP_K):
            row_copy(i + GATHER_AHEAD, r, k, nxt).start()
    route = r_ref[...]
    moe = route[:, 2:3] * y0 + route[:, 3:4] * y1
    z = ALPHA * x_ref[...] + gate_ref[0] * moe
    o_ref[...] = _layer_norm(z, g_ref[...], b_ref[...])

    @pl.when(i == nt - 1)
    def _():
        for t in range(nt, nt + GATHER_AHEAD):
            wait_tile(t % n_slots)


def moe_combine_postnorm(seg, pos, y_sorted, route, x, gate, ln_g, ln_b):
    kern = functools.partial(_combine_kernel, tm=seg.tm, nt=seg.nt)
    pos = jnp.concatenate([pos, jnp.zeros((GATHER_AHEAD * seg.tm * TOP_K,), jnp.int32)])
    row = lambda i, ps: (0, 0)
    grid_spec = pltpu.PrefetchScalarGridSpec(
        num_scalar_prefetch=1,
        grid=(seg.nt,),
        in_specs=[
            pl.BlockSpec(memory_space=pl.ANY),
            pl.BlockSpec((seg.tm, ROUTE_LANES), lambda i, ps: (i, 0)),
            pl.BlockSpec((seg.tm, D_MODEL), lambda i, ps: (i, 0)),
            pl.BlockSpec(seg.mod_block(), lambda i, ps: seg.mod_index(i)),
            pl.BlockSpec((1, D_MODEL), row),
            pl.BlockSpec((1, D_MODEL), row),
        ],
        out_specs=pl.BlockSpec((seg.tm, D_MODEL), lambda i, ps: (i, 0)),
        scratch_shapes=[pltpu.VMEM((GATHER_AHEAD + 1, TOP_K, seg.tm, D_MODEL), F32),
                        pltpu.SemaphoreType.DMA((GATHER_AHEAD + 1,))],
    )
    return pl.pallas_call(
        kern,
        out_shape=jax.ShapeDtypeStruct((seg.rows, D_MODEL), F32),
        grid_spec=grid_spec,
        compiler_params=_params("arbitrary"),
        name="moe_combine_postnorm",
    )(pos, y_sorted, route, x, gate, ln_g, ln_b)


def _rope_tables(pos):
    half = HEAD_DIM // 2
    inv = ROPE_THETA ** (-jnp.arange(half, dtype=F32) / half)
    ang = pos.astype(F32)[:, None] * inv[None, :]
    cos = jnp.cos(ang)
    sin = jnp.sin(ang)
    return jnp.concatenate([cos, cos], axis=-1), jnp.concatenate([-sin, sin], axis=-1)


def kernel(x_prompt, x_sample, cache_k, cache_v, state_conv, page_table, c_prompt, c_sample, ada_w, ada_b, ln_g, ln_b, conv_w_in, conv_b_in, conv_dw, conv_dw_b, conv_ln_g, conv_ln_b, conv_w_out, conv_b_out, attn_w_qkv, attn_lambda, attn_subln_g, attn_w_out, moe_w_group, moe_b_group, moe_w_expert, moe_b_expert, moe_w_in, moe_w_out):
    bp, tp, d = x_prompt.shape
    bs, ts, _ = x_sample.shape
    n_pages = page_table.shape[1]
    past_len = n_pages * PAGE_SIZE
    mp, ms = bp * tp, bs * ts
    assert d == D_MODEL and tp % 512 == 0 and ms % 8 == 0

    seg_p = Seg(mp, 512, tp, per_row=False)
    seg_p256 = Seg(mp, 256, tp, per_row=False)
    seg_s = Seg(ms, ms, ts, per_row=True)

    n_c = bp + bs
    c_rows = -(-n_c // 8) * 8
    c_pad = jnp.concatenate([c_prompt, c_sample, jnp.zeros((c_rows - n_c, d), F32)], axis=0)
    mods = ada_all(c_pad, ada_w.reshape(DEPTH * 2, d, 3 * d), ada_b.reshape(DEPTH * 2, 1, 3 * d))

    def mod_params(i, sub):
        m = mods[i * 2 + sub]
        out_p = [m[:bp, c * d:(c + 1) * d].reshape(bp, 1, d) for c in range(3)]
        out_s = [jnp.repeat(m[bp:n_c, c * d:(c + 1) * d], ts, axis=0).reshape(1, ms, d) for c in range(3)]
        return out_p, out_s

    cos_p, sin_p = _rope_tables(jnp.arange(tp, dtype=jnp.int32))
    cos_s, sin_s = _rope_tables(jnp.tile(past_len + jnp.arange(ts, dtype=jnp.int32), bs))

    x_p = x_prompt.reshape(mp, d)
    x_s = x_sample.reshape(ms, d)
    n_attn, pool = cache_k.shape[:2]
    n_hj = 2 * N_HEADS
    cache_k2 = cache_k.reshape(n_attn, pool, PAGE_SIZE * n_hj, HEAD_DIM)
    cache_v2 = cache_v.reshape(n_attn, pool, PAGE_SIZE * N_HEADS, V_DIM)

    n_assign = (mp + ms) * TOP_K
    n_tiles = -(-n_assign // MOE_TM) + N_EXPERTS - 1

    k_p = k_s = None
    v_p_rows, v_s_rows, conv_p_rows, conv_s_rows = [], [], [], []
    row2 = lambda a: a.reshape(1, -1)

    for i in range(DEPTH):
        l = i // N_MIXERS
        (sh_p, sc_p, gt_p), (sh_s, sc_s, gt_s) = mod_params(i, 0)
        lg, lb = row2(ln_g[i, 0]), row2(ln_b[i, 0])
        if i % N_MIXERS == 0:
            w_in = conv_w_in[l].astype(BF16)
            b_in = row2(conv_b_in[l])
            w_out = conv_w_out[l].astype(BF16)
            b_out = row2(conv_b_out[l])
            dw = jnp.concatenate([conv_dw[l], jnp.zeros((HALO - CONV_WIDTH, d), F32)], axis=0)
            cw = (dw, row2(conv_dw_b[l]), row2(conv_ln_g[l]), row2(conv_ln_b[l]))

            glu_p = mod_mm_glu(seg_p, x_p, sh_p, sc_p, w_in, b_in).reshape(bp, tp, d)
            tt = 256
            halo_idx = lambda b, t: (b, jnp.maximum(t * (tt // HALO) - 1, 0), 0)
            y_p = dwconv_ln_silu(glu_p, glu_p, halo_idx, True, tt, *cw).reshape(mp, d)
            conv_p_rows.append(glu_p[:, tp - (CONV_WIDTH - 1):])
            x_p = mm_postnorm(seg_p256, y_p, w_out, b_out, x_p, gt_p, lg, lb)

            glu_s = mod_mm_glu(seg_s, x_s, sh_s, sc_s, w_in, b_in).reshape(bs, ts, d)
            hist = state_conv[l].astype(F32)
            halo_s = jnp.concatenate([jnp.zeros((bs, HALO - (CONV_WIDTH - 1), d), F32), hist], axis=1)
            cur_s = jnp.concatenate([glu_s, jnp.zeros((bs, 8 - ts, d), F32)], axis=1)
            y_s = dwconv_ln_silu(cur_s, halo_s, lambda b, t: (b, 0, 0), False, 8, *cw)[:, :ts].reshape(ms, d)
            conv_s_rows.append(jnp.concatenate([hist, glu_s], axis=1)[:, ts:])
            x_s = mm_postnorm(seg_s, y_s, w_out, b_out, x_s, gt_s, lg, lb)
        else:
            lam_init = 0.8 - 0.6 * math.exp(-0.3 * i)
            w_qkv = attn_w_qkv[l].astype(BF16)
            w_out = attn_w_out[l].astype(BF16)
            no_bias = jnp.zeros((1, d), F32)
            lam_vecs = attn_lambda[l].astype(F32)
            sub_g = row2(attn_subln_g[l])

            k_p, v_p, qkv_p_bf = mod_mm_qkv(seg_p, l, n_attn, x_p, sh_p, sc_p, w_qkv, cos_p, sin_p, prev_k=k_p)
            v_p_rows.append(v_p.reshape(bp, tp, N_HEADS, V_DIM))
            o_p = flash_diff_attention(qkv_p_bf, bp, tp, lam_vecs, sub_g, lam_init)
            x_p = mm_postnorm(seg_p256, o_p, w_out, no_bias, x_p, gt_p, lg, lb)

            k_s, v_s, qkv_s_bf = mod_mm_qkv(seg_s, l, n_attn, x_s, sh_s, sc_s, w_qkv, cos_s, sin_s, prev_k=k_s)
            v_s_rows.append(v_s.reshape(bs, ts, N_HEADS, V_DIM))
            q_rows = qkv_s_bf[0].astype(F32).reshape(bs, ts, n_hj, HEAD_DIM).transpose(0, 2, 1, 3)
            q_rows = jnp.pad(q_rows, ((0, 0), (0, 0), (0, Q_ROWS - ts), (0, 0))).reshape(bs, n_hj * Q_ROWS, HEAD_DIM)
            k_new = jnp.pad(k_s[l].reshape(bs, ts * n_hj, HEAD_DIM), ((0, 0), (0, (PAGE_SIZE - ts) * n_hj), (0, 0)))
            v_new = jnp.pad(v_s.reshape(bs, ts * N_HEADS, V_DIM), ((0, 0), (0, (PAGE_SIZE - ts) * N_HEADS), (0, 0)))
            o_s = paged_diff_attention(l, page_table, q_rows, cache_k2, cache_v2, k_new, v_new, lam_vecs, sub_g, lam_init)
            o_s = o_s.reshape(bs, N_HEADS, Q_ROWS, V_DIM)[:, :, :ts].transpose(0, 2, 1, 3).reshape(ms, d)
            x_s = mm_postnorm(seg_s, o_s, w_out, no_bias, x_s, gt_s, lg, lb)

        (sh_p, sc_p, gt_p), (sh_s, sc_s, gt_s) = mod_params(i, 1)
        lg, lb = row2(ln_g[i, 1]), row2(ln_b[i, 1])
        n_logit = N_GROUPS + N_EXPERTS
        w_route = jnp.concatenate([moe_w_group[i], moe_w_expert[i], jnp.zeros((d, ROUTE_LANES - n_logit), F32)], axis=1)
        b_route = jnp.concatenate([moe_b_group[i], moe_b_expert[i], jnp.zeros((ROUTE_LANES - n_logit,), F32)]).reshape(1, -1)
        routed = moe_router(seg_p, mp + ms, 0, x_p, sh_p, sc_p, w_route, b_route)
        h_all, route = moe_router(seg_s, mp + ms, mp, x_s, sh_s, sc_s, w_route, b_route, prev=routed)
        ids = route[:, :TOP_K].astype(jnp.int32)
        pos, src_rows, tile_e, tile_valid, tile_first = route_layout(ids, n_tiles)
        y_sorted = moe_experts(i, tile_e, tile_valid, tile_first, src_rows, h_all, moe_w_in, moe_w_out)
        x_p = moe_combine_postnorm(seg_p256, pos[:mp * TOP_K], y_sorted, route[:mp], x_p, gt_p, lg, lb)
        x_s = moe_combine_postnorm(seg_s, pos[mp * TOP_K:], y_sorted, route[mp:], x_s, gt_s, lg, lb)

    return (x_p.reshape(bp, tp, d), x_s.reshape(bs, ts, d),
            k_p.reshape(n_attn, bp, tp, N_HEADS, 2, HEAD_DIM), jnp.stack(v_p_rows), jnp.stack(conv_p_rows),
            k_s.reshape(n_attn, bs, ts, N_HEADS, 2, HEAD_DIM), jnp.stack(v_s_rows), jnp.stack(conv_s_rows))
```

```python
import functools
import math

import jax
import jax.numpy as jnp
from jax import lax
from jax.experimental import pallas as pl
from jax.experimental.pallas import tpu as pltpu

D_MODEL = 2048
DEPTH = 4
PAGE_SIZE = 128
N_MIXERS = 2
CONV_WIDTH = 31
N_HEADS = 8
HEAD_DIM = D_MODEL // (2 * N_HEADS)
V_DIM = 2 * HEAD_DIM
ROPE_THETA = 10000.0
N_GROUPS = 4
EXPERTS_PER_GROUP = 8
N_EXPERTS = N_GROUPS * EXPERTS_PER_GROUP
TOP_K = 2
D_EXPERT = D_MODEL // 4
ALPHA = (2 * DEPTH) ** 0.25
LN_EPS = 1e-5

F32 = jnp.float32
BF16 = jnp.bfloat16
LANES = 128
HALO = 32
VMEM_LIMIT = 56 * 1024 * 1024
MOE_TM = 256
ROUTE_LANES = 128
NEG_INF = float("-inf")
Q_SCALE = HEAD_DIM ** -0.5 * math.log2(math.e)
DEC_PAGES = 8
Q_ROWS = 8
GATHER_AHEAD = 2


def _params(*sem):
    return pltpu.CompilerParams(dimension_semantics=sem, vmem_limit_bytes=VMEM_LIMIT)


def _dot(a, b):
    return jnp.dot(a, b, preferred_element_type=F32)


def _dot_nt(a, b):
    return lax.dot_general(a, b, (((1,), (1,)), ((), ())), preferred_element_type=F32)


def _layer_norm(z, g, b):
    mu = jnp.mean(z, axis=-1, keepdims=True)
    zc = z - mu
    var = jnp.mean(zc * zc, axis=-1, keepdims=True)
    return zc * lax.rsqrt(var + LN_EPS) * g + b


class Seg:
    def __init__(self, rows, tm, rows_per_batch, per_row):
        self.rows = rows
        self.tm = tm
        self.nt = rows // tm
        self.per_row = per_row
        self.tiles_per_batch = max(rows_per_batch // tm, 1)

    def mod_block(self):
        return (1, self.tm if self.per_row else 1, D_MODEL)

    def mod_index(self, i):
        return (0 if self.per_row else i // self.tiles_per_batch, 0, 0)

    def pos_index(self, i):
        return (0 if self.per_row else i % self.tiles_per_batch, 0)


def _ada_kernel(c_ref, w_ref, b_ref, o_ref):
    o_ref[0] = _dot(c_ref[...].astype(BF16), w_ref[0].astype(BF16)) + b_ref[0]


def ada_all(c_pad, w, b):
    n_sub, _, n_out = w.shape
    rows = c_pad.shape[0]
    tn = 1024
    return pl.pallas_call(
        _ada_kernel,
        out_shape=jax.ShapeDtypeStruct((n_sub, rows, n_out), F32),
        grid=(n_sub, n_out // tn),
        in_specs=[
            pl.BlockSpec((rows, D_MODEL), lambda s, j: (0, 0)),
            pl.BlockSpec((1, D_MODEL, tn), lambda s, j: (s, 0, j)),
            pl.BlockSpec((1, 1, tn), lambda s, j: (s, 0, j)),
        ],
        out_specs=pl.BlockSpec((1, rows, tn), lambda s, j: (s, 0, j)),
        compiler_params=_params("arbitrary", "arbitrary"),
        name="ada_all",
    )(c_pad, w, b)


def _glu_kernel(x_ref, sh_ref, sc_ref, wa_ref, wg_ref, ba_ref, bg_ref, o_ref, h_scr):
    @pl.when(pl.program_id(1) == 0)
    def _():
        h_scr[...] = (x_ref[...] * (1.0 + sc_ref[0]) + sh_ref[0]).astype(BF16)

    h = h_scr[...]
    a = _dot(h, wa_ref[...]) + ba_ref[...]
    g = _dot(h, wg_ref[...]) + bg_ref[...]
    o_ref[...] = a * jax.nn.sigmoid(g)


def mod_mm_glu(seg, x, shift, scale, w_in, b_in):
    tn = 512
    nj = D_MODEL // tn
    return pl.pallas_call(
        _glu_kernel,
        out_shape=jax.ShapeDtypeStruct((seg.rows, D_MODEL), F32),
        grid=(seg.nt, nj),
        in_specs=[
            pl.BlockSpec((seg.tm, D_MODEL), lambda i, j: (i, 0)),
            pl.BlockSpec(seg.mod_block(), lambda i, j: seg.mod_index(i)),
            pl.BlockSpec(seg.mod_block(), lambda i, j: seg.mod_index(i)),
            pl.BlockSpec((D_MODEL, tn), lambda i, j: (0, j)),
            pl.BlockSpec((D_MODEL, tn), lambda i, j: (0, j + nj)),
            pl.BlockSpec((1, tn), lambda i, j: (0, j)),
            pl.BlockSpec((1, tn), lambda i, j: (0, j + nj)),
        ],
        out_specs=pl.BlockSpec((seg.tm, tn), lambda i, j: (i, j)),
        scratch_shapes=[pltpu.VMEM((seg.tm, D_MODEL), BF16)],
        compiler_params=_params("arbitrary", "arbitrary"),
        name="mod_mm_glu",
    )(x, shift, scale, w_in, w_in, b_in, b_in)


def _dwconv_kernel(cur_ref, halo_ref, dw_ref, dwb_ref, g_ref, b_ref, o_ref, xs_scr, sh_scr, y_scr, *, tt, zero_first):
    hist = halo_ref[0]
    if zero_first:
        hist = jnp.where(pl.program_id(1) == 0, jnp.zeros_like(hist), hist)
    xs_scr[0:HALO, :] = hist
    xs_scr[HALO:HALO + tt, :] = cur_ref[0]

    rch = min(tt, 32)
    cch = sh_scr.shape[2]
    span = sh_scr.shape[1]
    lead = HALO - (CONV_WIDTH - 1)

    def col_body(cc, carry):
        cols = pl.ds(pl.multiple_of(cc * cch, cch), cch)
        for r in range(1, 8):
            sh_scr[r - 1] = xs_scr[r:r + span, cols]
        for rc in range(tt // rch):
            acc = jnp.zeros((rch, cch), F32)
            for j in range(CONV_WIDTH):
                q, r = divmod(lead + j, 8)
                u0 = rc * rch + 8 * q
                win = xs_scr[u0:u0 + rch, cols] if r == 0 else sh_scr[r - 1, u0:u0 + rch, :]
                acc = acc + dw_ref[j:j + 1, cols] * win
            y_scr[rc * rch:(rc + 1) * rch, cols] = acc + dwb_ref[:, cols]
        return carry

    lax.fori_loop(0, D_MODEL // cch, col_body, 0)
    y = _layer_norm(y_scr[...], g_ref[...], b_ref[...])
    o_ref[0] = (y * jax.nn.sigmoid(y)).astype(BF16)


def dwconv_ln_silu(cur, halo, halo_index, zero_first, tt, dw, dw_b, ln_g, ln_b):
    nb, t, _ = cur.shape
    kern = functools.partial(_dwconv_kernel, tt=tt, zero_first=zero_first)
    return pl.pallas_call(
        kern,
        out_shape=jax.ShapeDtypeStruct((nb, t, D_MODEL), BF16),
        grid=(nb, t // tt),
        in_specs=[
            pl.BlockSpec((1, tt, D_MODEL), lambda b, i: (b, i, 0)),
            pl.BlockSpec((1, HALO, D_MODEL), halo_index),
            pl.BlockSpec((HALO, D_MODEL), lambda b, i: (0, 0)),
            pl.BlockSpec((1, D_MODEL), lambda b, i: (0, 0)),
            pl.BlockSpec((1, D_MODEL), lambda b, i: (0, 0)),
            pl.BlockSpec((1, D_MODEL), lambda b, i: (0, 0)),
        ],
        out_specs=pl.BlockSpec((1, tt, D_MODEL), lambda b, i: (b, i, 0)),
        scratch_shapes=[pltpu.VMEM((HALO + tt, D_MODEL), F32),
                        pltpu.VMEM((7, tt + HALO - 8, 256), F32),
                        pltpu.VMEM((tt, D_MODEL), F32)],
        compiler_params=_params("arbitrary", "arbitrary"),
        name="dwconv_ln_silu",
    )(cur, halo, dw, dw_b, ln_g, ln_b)


def _mm_postnorm_kernel(y_ref, w_ref, bias_ref, x_ref, gate_ref, g_ref, b_ref, o_ref):
    tm = y_ref.shape[0]
    rc = min(tm, 128)
    gate = gate_ref[0]
    for r in range(tm // rc):
        rows = slice(r * rc, (r + 1) * rc)
        out = _dot(y_ref[rows, :], w_ref[...]) + bias_ref[...]
        z = ALPHA * x_ref[rows, :] + (gate if gate.shape[0] == 1 else gate[rows]) * out
        o_ref[rows, :] = _layer_norm(z, g_ref[...], b_ref[...])


def mm_postnorm(seg, y, w, bias, x, gate, ln_g, ln_b):
    k = y.shape[1]
    row = lambda i: (0, 0)
    return pl.pallas_call(
        _mm_postnorm_kernel,
        out_shape=jax.ShapeDtypeStruct((seg.rows, D_MODEL), F32),
        grid=(seg.nt,),
        in_specs=[
            pl.BlockSpec((seg.tm, k), lambda i: (i, 0)),
            pl.BlockSpec((k, D_MODEL), row),
            pl.BlockSpec((1, D_MODEL), row),
            pl.BlockSpec((seg.tm, D_MODEL), lambda i: (i, 0)),
            pl.BlockSpec(seg.mod_block(), seg.mod_index),
            pl.BlockSpec((1, D_MODEL), row),
            pl.BlockSpec((1, D_MODEL), row),
        ],
        out_specs=pl.BlockSpec((seg.tm, D_MODEL), lambda i: (i, 0)),
        compiler_params=_params("arbitrary"),
        name="mm_postnorm",
    )(y, w, bias, x, gate, ln_g, ln_b)


def _qkv_kernel(x_ref, sh_ref, sc_ref, w_ref, cos_ref, sin_ref, *rest, tm, n_fill):
    ok_ref, ov_ref, ob_ref, h_scr = rest[-4:]
    part = pl.program_id(1) - n_fill
    cw = 4 * HEAD_DIM
    n_hj = D_MODEL // HEAD_DIM

    def rope_chunks(store_rot):
        h = h_scr[...]
        cos = cos_ref[...]
        sin = sin_ref[...]
        for c in range(D_MODEL // cw):
            acc = _dot(h, w_ref[:, c * cw:(c + 1) * cw])
            for g in range(cw // HEAD_DIM):
                blk = acc[:, g * HEAD_DIM:(g + 1) * HEAD_DIM]
                rot = blk * cos + pltpu.roll(blk, HEAD_DIM // 2, 1) * sin
                store_rot(c * (cw // HEAD_DIM) + g, rot)

    @pl.when(part == 0)
    def _():
        h_scr[...] = (x_ref[...] * (1.0 + sc_ref[0]) + sh_ref[0]).astype(BF16)

        def store_q(hj, rot):
            ob_ref[0, :, hj * HEAD_DIM:(hj + 1) * HEAD_DIM] = (rot * Q_SCALE).astype(BF16)
        rope_chunks(store_q)

    @pl.when(part == 1)
    def _():
        def store_k(hj, rot):
            ob_ref[0, :, hj * HEAD_DIM:(hj + 1) * HEAD_DIM] = rot.astype(BF16)
            ok_ref[0, pl.ds(hj, tm, stride=n_hj), :] = rot
        rope_chunks(store_k)

    @pl.when(part == 2)
    def _():
        h = h_scr[...]
        for c in range(D_MODEL // cw):
            acc = _dot(h, w_ref[:, c * cw:(c + 1) * cw])
            ob_ref[0, :, c * cw:(c + 1) * cw] = acc.astype(BF16)
            ov_ref[:, c * cw:(c + 1) * cw] = acc

    @pl.when(part < 0)
    def _():
        ok_ref[...] = jnp.zeros_like(ok_ref)


def mod_mm_qkv(seg, layer, n_layers, x, shift, scale, w_qkv, cos, sin, prev_k=None):
    n_hj = D_MODEL // HEAD_DIM
    aliased = prev_k is not None
    n_fill = 0 if aliased else n_layers - 1
    part = lambda p: jnp.clip(p - n_fill, 0, 2)

    def k_slab(p):
        return jnp.where(p >= n_fill, layer, jnp.where(p < layer, p, p + 1))

    in_specs = [
        pl.BlockSpec((seg.tm, D_MODEL), lambda i, p: (i, 0)),
        pl.BlockSpec(seg.mod_block(), lambda i, p: seg.mod_index(i)),
        pl.BlockSpec(seg.mod_block(), lambda i, p: seg.mod_index(i)),
        pl.BlockSpec((D_MODEL, D_MODEL), lambda i, p: (0, part(p))),
        pl.BlockSpec((seg.tm, HEAD_DIM), lambda i, p: seg.pos_index(i)),
        pl.BlockSpec((seg.tm, HEAD_DIM), lambda i, p: seg.pos_index(i)),
    ]
    args = [x, shift, scale, w_qkv, cos, sin]
    aliases = {}
    if aliased:
        in_specs.append(pl.BlockSpec(memory_space=pl.ANY))
        aliases = {len(args): 0}
        args.append(prev_k)
    return pl.pallas_call(
        functools.partial(_qkv_kernel, tm=seg.tm, n_fill=n_fill),
        out_shape=(jax.ShapeDtypeStruct((n_layers, seg.rows * n_hj, HEAD_DIM), F32),
                   jax.ShapeDtypeStruct((seg.rows, D_MODEL), F32),
                   jax.ShapeDtypeStruct((3, seg.rows, D_MODEL), BF16)),
        grid=(seg.nt, 3 + n_fill),
        in_specs=in_specs,
        out_specs=(pl.BlockSpec((1, seg.tm * n_hj, HEAD_DIM), lambda i, p: (k_slab(p), i, 0)),
                   pl.BlockSpec((seg.tm, D_MODEL), lambda i, p: (i, 0)),
                   pl.BlockSpec((1, seg.tm, D_MODEL), lambda i, p: (part(p), i, 0))),
        scratch_shapes=[pltpu.VMEM((seg.tm, D_MODEL), BF16)],
        input_output_aliases=aliases,
        compiler_params=_params("arbitrary", "arbitrary"),
        name="mod_mm_qkv",
    )(*args)


def _diff_lambda(lam_ref, lam_init):
    lv = lam_ref[...]
    s01 = jnp.sum(lv[0:1] * lv[1:2], axis=-1, keepdims=True)
    s23 = jnp.sum(lv[2:3] * lv[3:4], axis=-1, keepdims=True)
    return jnp.exp(s01) - jnp.exp(s23) + lam_init


def _head_rms(o, g, lam_init):
    ms = jnp.mean(o * o, axis=-1, keepdims=True)
    return o * lax.rsqrt(ms + LN_EPS) * g * (1.0 - lam_init)


def _flash_kernel(qi_ref, ki_ref, q_ref, k_ref, v_ref, lam_ref, g_ref, o_ref, m_scr, l_scr, acc_scr, s_scr, p_scr, a_scr,
                  *, tq, lam_init):
    pr = pl.program_id(2)
    qi = qi_ref[pr]
    ki = ki_ref[pr]
    rc = 32
    n_kt = tq // LANES

    @pl.when(ki == 0)
    def _():
        m_scr[...] = jnp.full_like(m_scr, NEG_INF)
        l_scr[...] = jnp.zeros_like(l_scr)
        acc_scr[...] = jnp.zeros_like(acc_scr)

    def step(masked):
        for j in range(2):
            lanes = slice(j * HEAD_DIM, (j + 1) * HEAD_DIM)
            s_scr[j] = _dot_nt(q_ref[0, :, lanes], k_ref[0, :, lanes])
        for j in range(2):
            def chunk(c):
                r0 = c * rc
                rows = slice(r0, r0 + rc)
                parts = [s_scr[j, rows, kt * LANES:(kt + 1) * LANES] for kt in range(n_kt)]
                if masked:
                    row = r0 + lax.broadcasted_iota(jnp.int32, (rc, LANES), 0)
                    col = lax.broadcasted_iota(jnp.int32, (rc, LANES), 1)
                    parts = [jnp.where(col + kt * LANES <= row, parts[kt], NEG_INF) for kt in range(n_kt)]
                mx = parts[0]
                for kt in range(1, n_kt):
                    mx = jnp.maximum(mx, parts[kt])
                m_old = m_scr[j, rows, :]
                m_new = jnp.maximum(m_old, jnp.max(mx, axis=1, keepdims=True))
                alpha = jnp.exp2(m_old - m_new)
                lsum = jnp.zeros((rc, LANES), F32)
                for kt in range(n_kt):
                    p = jnp.exp2(parts[kt] - m_new)
                    lsum = lsum + p
                    p_scr[j, rows, kt * LANES:(kt + 1) * LANES] = p.astype(BF16)
                l_scr[j, rows, :] = alpha * l_scr[j, rows, :] + lsum
                m_scr[j, rows, :] = m_new
                a_scr[j, rows, :] = alpha

            for c in range(tq // rc):
                chunk(c)
            alpha = a_scr[j]
            acc_scr[j] = jnp.concatenate([alpha, alpha], axis=1) * acc_scr[j] + _dot(p_scr[j], v_ref[0])

    @pl.when(ki < qi)
    def _():
        step(False)

    @pl.when(ki == qi)
    def _():
        step(True)
        lam = _diff_lambda(lam_ref, lam_init)
        l0 = jnp.sum(l_scr[0], axis=1, keepdims=True)
        l1 = jnp.sum(l_scr[1], axis=1, keepdims=True)
        o = acc_scr[0] / l0 - lam * (acc_scr[1] / l1)
        o_ref[...] = _head_rms(o, g_ref[...], lam_init).astype(BF16)


def flash_diff_attention(qkv_bf, nb, t, lam_vecs, subln_g, lam_init):
    tq = 512
    nq = t // tq
    pairs = [(a, b) for a in range(nq) for b in range(a + 1)]
    qi_tab = jnp.asarray([p[0] for p in pairs], jnp.int32)
    ki_tab = jnp.asarray([p[1] for p in pairs], jnp.int32)
    kern = functools.partial(_flash_kernel, tq=tq, lam_init=lam_init)
    grid_spec = pltpu.PrefetchScalarGridSpec(
        num_scalar_prefetch=2,
        grid=(nb, N_HEADS, len(pairs)),
        in_specs=[
            pl.BlockSpec((1, tq, V_DIM), lambda b, h, p, qt, kt: (0, b * nq + qt[p], h)),
            pl.BlockSpec((1, tq, V_DIM), lambda b, h, p, qt, kt: (1, b * nq + kt[p], h)),
            pl.BlockSpec((1, tq, V_DIM), lambda b, h, p, qt, kt: (2, b * nq + kt[p], h)),
            pl.BlockSpec((4, HEAD_DIM), lambda b, h, p, qt, kt: (0, 0)),
            pl.BlockSpec((1, V_DIM), lambda b, h, p, qt, kt: (0, 0)),
        ],
        out_specs=pl.BlockSpec((tq, V_DIM), lambda b, h, p, qt, kt: (b * nq + qt[p], h)),
        scratch_shapes=[
            pltpu.VMEM((2, tq, LANES), F32),
            pltpu.VMEM((2, tq, LANES), F32),
            pltpu.VMEM((2, tq, V_DIM), F32),
            pltpu.VMEM((2, tq, tq), F32),
            pltpu.VMEM((2, tq, tq), BF16),
            pltpu.VMEM((2, tq, LANES), F32),
        ],
    )
    return pl.pallas_call(
        kern,
        out_shape=jax.ShapeDtypeStruct((nb * t, D_MODEL), BF16),
        grid_spec=grid_spec,
        compiler_params=_params("arbitrary", "arbitrary", "arbitrary"),
        name="flash_diff_attention",
    )(qi_tab, ki_tab, qkv_bf, qkv_bf, qkv_bf, lam_vecs, subln_g)


def _decode_kernel(pt_ref, q_ref, *refs, n_steps, lam_init):
    k_refs = refs[:DEC_PAGES]
    v_refs = refs[DEC_PAGES:2 * DEC_PAGES]
    kn_ref, vn_ref, spread_ref, lam_ref, g_ref, o_ref, m_scr, l_scr, acc_scr, own_scr = refs[2 * DEC_PAGES:]
    p = pl.program_id(1)
    n_hj = 2 * N_HEADS
    grp = 2 * Q_ROWS

    @pl.when(p == 0)
    def _():
        m_scr[...] = jnp.full_like(m_scr, NEG_INF)
        l_scr[...] = jnp.zeros_like(l_scr)
        acc_scr[...] = jnp.zeros_like(acc_scr)
        row_head = lax.broadcasted_iota(jnp.int32, own_scr.shape, 0) // grp
        col_head = lax.broadcasted_iota(jnp.int32, own_scr.shape, 1) % N_HEADS
        own_scr[...] = (row_head == col_head).astype(F32)

    def process(k_pages, v_pages, keep):
        q_parts = [q_ref[0, hj * Q_ROWS:(hj + 1) * Q_ROWS, :].astype(BF16) for hj in range(n_hj)]
        s_pages = []
        for k_ref in k_pages:
            parts = []
            for hj in range(n_hj):
                k_hj = k_ref[pl.ds(hj, PAGE_SIZE, stride=n_hj), :].astype(BF16)
                parts.append(_dot_nt(q_parts[hj], k_hj))
            s_pages.append(jnp.concatenate(parts, axis=0))
        s = jnp.concatenate(s_pages, axis=1)
        if keep is not None:
            s = jnp.where(keep, s, NEG_INF)
        m_old = m_scr[...]
        m_new = jnp.maximum(m_old, jnp.max(s, axis=1, keepdims=True))
        alpha = jnp.exp2(m_old - m_new)
        pe = jnp.exp2(s - m_new)
        l_scr[...] = alpha * l_scr[...] + jnp.sum(pe, axis=1, keepdims=True)
        m_scr[...] = m_new
        own = own_scr[...]
        pv = None
        for c, v_ref in enumerate(v_pages):
            pe_c = pe[:, c * PAGE_SIZE:(c + 1) * PAGE_SIZE].astype(BF16)
            spread = _dot(pe_c, spread_ref[...]) * own
            term = _dot(spread.astype(BF16), v_ref[...].astype(BF16))
            pv = term if pv is None else pv + term
        acc_scr[...] = alpha * acc_scr[...] + pv

    process([r.at[0, 0] for r in k_refs], [r.at[0, 0] for r in v_refs], None)

    @pl.when(p == n_steps - 1)
    def _():
        row = lax.broadcasted_iota(jnp.int32, (n_hj * Q_ROWS, PAGE_SIZE), 0)
        col = lax.broadcasted_iota(jnp.int32, (n_hj * Q_ROWS, PAGE_SIZE), 1)
        process([kn_ref.at[0]], [vn_ref.at[0]], col <= row % Q_ROWS)

        lam = _diff_lambda(lam_ref, lam_init)
        an = acc_scr[...] / l_scr[...]
        outs = []
        for h in range(N_HEADS):
            outs.append(an[h * grp:h * grp + Q_ROWS] - lam * an[h * grp + Q_ROWS:(h + 1) * grp])
        o = jnp.concatenate(outs, axis=0)
        o_ref[0] = _head_rms(o, g_ref[...], lam_init).astype(BF16)


def paged_diff_attention(layer, page_table, q_rows, cache_k, cache_v, k_new, v_new, lam_vecs, subln_g, lam_init):
    nb, n_pages = page_table.shape
    assert n_pages % DEC_PAGES == 0
    n_steps = n_pages // DEC_PAGES
    n_hj = 2 * N_HEADS
    kern = functools.partial(_decode_kernel, n_steps=n_steps, lam_init=lam_init)

    def page_idx(c):
        return lambda b, p, pt: (layer, pt[b * n_pages + p * DEC_PAGES + c], 0, 0)

    fixed = lambda b, p, pt: (b, 0, 0)
    grid_spec = pltpu.PrefetchScalarGridSpec(
        num_scalar_prefetch=1,
        grid=(nb, n_steps),
        in_specs=(
            [pl.BlockSpec((1, n_hj * Q_ROWS, HEAD_DIM), fixed)]
            + [pl.BlockSpec((1, 1, PAGE_SIZE * n_hj, HEAD_DIM), page_idx(c)) for c in range(DEC_PAGES)]
            + [pl.BlockSpec((1, 1, PAGE_SIZE * N_HEADS, V_DIM), page_idx(c)) for c in range(DEC_PAGES)]
            + [pl.BlockSpec((1, PAGE_SIZE * n_hj, HEAD_DIM), fixed),
               pl.BlockSpec((1, PAGE_SIZE * N_HEADS, V_DIM), fixed),
               pl.BlockSpec((PAGE_SIZE, PAGE_SIZE * N_HEADS), lambda b, p, pt: (0, 0)),
               pl.BlockSpec((4, HEAD_DIM), lambda b, p, pt: (0, 0)),
               pl.BlockSpec((1, V_DIM), lambda b, p, pt: (0, 0))]),
        out_specs=pl.BlockSpec((1, N_HEADS * Q_ROWS, V_DIM), fixed),
        scratch_shapes=[
            pltpu.VMEM((n_hj * Q_ROWS, 1), F32),
            pltpu.VMEM((n_hj * Q_ROWS, 1), F32),
            pltpu.VMEM((n_hj * Q_ROWS, V_DIM), F32),
            pltpu.VMEM((n_hj * Q_ROWS, PAGE_SIZE * N_HEADS), F32),
        ],
    )
    spread = jnp.repeat(jnp.eye(PAGE_SIZE, dtype=BF16), N_HEADS, axis=1)
    return pl.pallas_call(
        kern,
        out_shape=jax.ShapeDtypeStruct((nb, N_HEADS * Q_ROWS, V_DIM), BF16),
        grid_spec=grid_spec,
        compiler_params=_params("arbitrary", "arbitrary"),
        name="paged_diff_attention",
    )(page_table.reshape(-1), q_rows, *([cache_k] * DEC_PAGES), *([cache_v] * DEC_PAGES), k_new, v_new,
      spread, lam_vecs, subln_g)


def _router_kernel(x_ref, sh_ref, sc_ref, w_ref, b_ref, *rest, n_real):
    h_ref, r_ref = rest[-2:]

    @pl.when(pl.program_id(0) >= n_real)
    def _():
        h_ref[...] = jnp.zeros_like(h_ref)
        r_ref[...] = jnp.zeros_like(r_ref)

    @pl.when(pl.program_id(0) < n_real)
    def _():
        _route_rows(x_ref, sh_ref, sc_ref, w_ref, b_ref, h_ref, r_ref)


def _route_rows(x_ref, sh_ref, sc_ref, w_ref, b_ref, h_ref, r_ref):
    h = x_ref[...] * (1.0 + sc_ref[0]) + sh_ref[0]
    h_ref[...] = h
    logits = _dot(h.astype(BF16), w_ref[...].astype(BF16)) + b_ref[...]

    lane = lax.broadcasted_iota(jnp.int32, logits.shape, 1).astype(F32)

    def first_max(vals):
        vmax = jnp.max(vals, axis=1, keepdims=True)
        idx = jnp.min(jnp.where(vals == vmax, lane, float(ROUTE_LANES)), axis=1, keepdims=True)
        return vmax, idx

    gl = jnp.where(lane < N_GROUPS, logits, NEG_INF)
    gmax, gsel = first_max(gl)
    pg_sel = 1.0 / jnp.sum(jnp.exp(gl - gmax), axis=1, keepdims=True)

    lo = N_GROUPS + EXPERTS_PER_GROUP * gsel
    el = jnp.where((lane >= lo) & (lane < lo + EXPERTS_PER_GROUP), logits, NEG_INF)
    v0, i0 = first_max(el)
    v1, i1 = first_max(jnp.where(lane == i0, NEG_INF, el))
    e1 = jnp.exp(v1 - v0)
    w0 = pg_sel / (1.0 + e1)
    w1 = pg_sel * e1 / (1.0 + e1)

    out = jnp.where(lane == 0, i0 - N_GROUPS, 0.0)
    out = jnp.where(lane == 1, i1 - N_GROUPS, out)
    out = jnp.where(lane == 2, w0, out)
    out = jnp.where(lane == 3, w1, out)
    r_ref[...] = out


def moe_router(seg, rows_total, row_offset, x, shift, scale, w_route, b_route, prev=None):
    assert row_offset % seg.tm == 0
    blk0 = row_offset // seg.tm
    n_real = seg.nt
    n_fill = 0
    if prev is None and rows_total > row_offset + seg.rows:
        assert row_offset == 0 and rows_total - seg.rows <= seg.tm
        n_fill = 1
    real = lambda i: jnp.minimum(i, n_real - 1)
    row = lambda i: (0, 0)
    in_specs = [
        pl.BlockSpec((seg.tm, D_MODEL), lambda i: (real(i), 0)),
        pl.BlockSpec(seg.mod_block(), lambda i: seg.mod_index(real(i))),
        pl.BlockSpec(seg.mod_block(), lambda i: seg.mod_index(real(i))),
        pl.BlockSpec((D_MODEL, ROUTE_LANES), row),
        pl.BlockSpec((1, ROUTE_LANES), row),
    ]
    args = [x, shift, scale, w_route, b_route]
    aliases = {}
    if prev is not None:
        in_specs += [pl.BlockSpec(memory_space=pl.ANY), pl.BlockSpec(memory_space=pl.ANY)]
        aliases = {len(args): 0, len(args) + 1: 1}
        args += list(prev)
    return pl.pallas_call(
        functools.partial(_router_kernel, n_real=n_real),
        out_shape=(jax.ShapeDtypeStruct((rows_total, D_MODEL), F32),
                   jax.ShapeDtypeStruct((rows_total, ROUTE_LANES), F32)),
        grid=(n_real + n_fill,),
        in_specs=in_specs,
        out_specs=(pl.BlockSpec((seg.tm, D_MODEL), lambda i: (blk0 + i, 0)),
                   pl.BlockSpec((seg.tm, ROUTE_LANES), lambda i: (blk0 + i, 0))),
        input_output_aliases=aliases,
        compiler_params=_params("arbitrary"),
        name="moe_router",
    )(*args)


def route_layout(expert_ids, n_tiles):
    n_assign = expert_ids.size
    e = expert_ids.reshape(n_assign)
    onehot = (e[:, None] == jnp.arange(N_EXPERTS, dtype=jnp.int32)[None, :]).astype(jnp.int32)
    csum = jnp.cumsum(onehot, axis=0)
    rank = jnp.sum(csum * onehot, axis=1) - 1
    counts = csum[-1]
    padded = ((counts + MOE_TM - 1) // MOE_TM) * MOE_TM
    ends = jnp.cumsum(padded)
    starts = ends - padded
    pos = starts[e] + rank
    src_rows = jnp.zeros(((n_tiles + GATHER_AHEAD) * MOE_TM,), jnp.int32).at[pos].set(
        jnp.arange(n_assign, dtype=jnp.int32) // TOP_K)
    tile_start = jnp.arange(n_tiles, dtype=jnp.int32) * MOE_TM
    valid = tile_start < ends[-1]
    tile_e = jnp.minimum(jnp.searchsorted(ends, tile_start, side="right"), N_EXPERTS - 1).astype(jnp.int32)
    last_e = jnp.max(jnp.where(valid, tile_e, 0))
    tile_e = jnp.where(valid, tile_e, last_e)
    prev_e = jnp.concatenate([jnp.full((1,), -1, jnp.int32), tile_e[:-1]])
    first = valid & (tile_e != prev_e)
    used = counts > 0
    used_experts = jnp.nonzero(used, size=N_EXPERTS, fill_value=0)[0].astype(jnp.int32)
    tile_k = (jnp.cumsum(used.astype(jnp.int32)) - 1)[tile_e]
    n_used = jnp.sum(used.astype(jnp.int32)).reshape(1)
    tiles = (valid.astype(jnp.int32), first.astype(jnp.int32), tile_k.astype(jnp.int32), used_experts, n_used)
    return pos.astype(jnp.int32), src_rows, tiles


def _moe_kernel(tv_ref, tf_ref, tk_ref, ek_ref, nk_ref, src_ref, h_hbm, win_hbm, wout_hbm, y_ref,
                buf, sem, wf_in, wf_out, wsem, win_b, wout_b, *, layer, n_tiles):
    i = pl.program_id(0)
    n_slots = buf.shape[0]
    slot = lax.rem(i, n_slots)
    n_used = nk_ref[0]

    def weight_copies(k, wslot):
        e = ek_ref[k]
        return (pltpu.make_async_copy(win_hbm.at[layer, e], wf_in.at[wslot], wsem.at[0, wslot]),
                pltpu.make_async_copy(wout_hbm.at[layer, e], wf_out.at[wslot], wsem.at[1, wslot]))

    def row_copy(tile, r, dst_slot):
        src = src_ref[tile * MOE_TM + r]
        return pltpu.make_async_copy(h_hbm.at[pl.ds(src, 1)], buf.at[dst_slot, pl.ds(r, 1)], sem.at[dst_slot])

    def wait_tile(dst_slot):
        pltpu.make_async_copy(h_hbm.at[pl.ds(0, MOE_TM)], buf.at[dst_slot], sem.at[dst_slot]).wait()

    def started_by_valid(tile):
        return tv_ref[jnp.maximum(tile - GATHER_AHEAD, 0)] == 1

    @pl.when(i == 0)
    def _():
        for t in range(GATHER_AHEAD):
            def body(r, carry):
                row_copy(t, r, t).start()
                return carry
            lax.fori_loop(0, MOE_TM, body, 0)
        for c in weight_copies(0, 0):
            c.start()

        @pl.when(n_used > 1)
        def _():
            for c in weight_copies(1, 1):
                c.start()

    @pl.when(tf_ref[i] == 1)
    def _():
        k = tk_ref[i]
        wslot = lax.rem(k, 2)
        for c in weight_copies(k, wslot):
            c.wait()
        win_b[...] = wf_in[wslot].astype(BF16)
        wout_b[...] = wf_out[wslot].astype(BF16)

        @pl.when(k + 2 < n_used)
        def _():
            for c in weight_copies(k + 2, wslot):
                c.start()

    @pl.when((i < GATHER_AHEAD) | started_by_valid(i))
    def _():
        wait_tile(slot)

    @pl.when(tv_ref[i] == 1)
    def _():
        x = buf[slot].astype(BF16)
        nxt = lax.rem(i + GATHER_AHEAD, n_slots)
        for r in range(MOE_TM):
            row_copy(i + GATHER_AHEAD, r, nxt).start()
        u = _dot(x, win_b[...])
        a = u[:, :D_EXPERT]
        act = a * jax.nn.sigmoid(a) * u[:, D_EXPERT:]
        y_ref[...] = _dot(act.astype(BF16), wout_b[...])

    @pl.when(tv_ref[i] == 0)
    def _():
        y_ref[...] = jnp.zeros_like(y_ref)

    @pl.when(i == n_tiles - 1)
    def _():
        for t in range(n_tiles, n_tiles + GATHER_AHEAD):
            @pl.when(started_by_valid(t))
            def _():
                wait_tile(t % n_slots)


def moe_experts(layer, tiles, src_rows, h_all, w_in, w_out):
    tile_valid, tile_first, tile_k, used_experts, n_used = tiles
    n_tiles = tile_valid.shape[0]
    kern = functools.partial(_moe_kernel, layer=layer, n_tiles=n_tiles)
    grid_spec = pltpu.PrefetchScalarGridSpec(
        num_scalar_prefetch=6,
        grid=(n_tiles,),
        in_specs=[pl.BlockSpec(memory_space=pl.ANY)] * 3,
        out_specs=pl.BlockSpec((MOE_TM, D_MODEL), lambda i, *_: (i, 0)),
        scratch_shapes=[
            pltpu.VMEM((GATHER_AHEAD + 1, MOE_TM, D_MODEL), F32),
            pltpu.SemaphoreType.DMA((GATHER_AHEAD + 1,)),
            pltpu.VMEM((2, D_MODEL, 2 * D_EXPERT), F32),
            pltpu.VMEM((2, D_EXPERT, D_MODEL), F32),
            pltpu.SemaphoreType.DMA((2, 2)),
            pltpu.VMEM((D_MODEL, 2 * D_EXPERT), BF16),
            pltpu.VMEM((D_EXPERT, D_MODEL), BF16),
        ],
    )
    return pl.pallas_call(
        kern,
        out_shape=jax.ShapeDtypeStruct((n_tiles * MOE_TM, D_MODEL), F32),
        grid_spec=grid_spec,
        compiler_params=_params("arbitrary"),
        name="moe_experts",
    )(tile_valid, tile_first, tile_k, used_experts, n_used, src_rows, h_all, w_in, w_out)


def _combine_kernel(pos_ref, y_hbm, r_ref, x_ref, gate_ref, g_ref, b_ref, o_ref, buf, sem, *, tm, nt):
    i = pl.program_id(0)
    n_slots = buf.shape[0]
    slot = lax.rem(i, n_slots)

    def row_copy(tile, r, k, dst_slot):
        src = pos_ref[(tile * tm + r) * TOP_K + k]
        return pltpu.make_async_copy(y_hbm.at[pl.ds(src, 1)], buf.at[dst_slot, k, pl.ds(r, 1)], sem.at[dst_slot])

    def wait_tile(dst_slot):
        for k in range(TOP_K):
            pltpu.make_async_copy(y_hbm.at[pl.ds(0, tm)], buf.at[dst_slot, k], sem.at[dst_slot]).wait()

    @pl.when(i == 0)
    def _():
        for t in range(GATHER_AHEAD):
            def body(r, carry):
                for k in range(TOP_K):
                    row_copy(t, r, k, t).start()
                return carry
            lax.fori_loop(0, tm, body, 0)

    wait_tile(slot)
    y0 = buf[slot, 0]
    y1 = buf[slot, 1]
    nxt = lax.rem(i + GATHER_AHEAD, n_slots)
    for r in range(tm):
        for k in range(TOP_K):
            row_copy(i + GATHER_AHEAD, r, k, nxt).start()
    route = r_ref[...]
    moe = route[:, 2:3] * y0 + route[:, 3:4] * y1
    z = ALPHA * x_ref[...] + gate_ref[0] * moe
    o_ref[...] = _layer_norm(z, g_ref[...], b_ref[...])

    @pl.when(i == nt - 1)
    def _():
        for t in range(nt, nt + GATHER_AHEAD):
            wait_tile(t % n_slots)


def moe_combine_postnorm(seg, pos, y_sorted, route, x, gate, ln_g, ln_b):
    kern = functools.partial(_combine_kernel, tm=seg.tm, nt=seg.nt)
    pos = jnp.concatenate([pos, jnp.zeros((GATHER_AHEAD * seg.tm * TOP_K,), jnp.int32)])
    row = lambda i, ps: (0, 0)
    grid_spec = pltpu.PrefetchScalarGridSpec(
        num_scalar_prefetch=1,
        grid=(seg.nt,),
        in_specs=[
            pl.BlockSpec(memory_space=pl.ANY),
            pl.BlockSpec((seg.tm, ROUTE_LANES), lambda i, ps: (i, 0)),
            pl.BlockSpec((seg.tm, D_MODEL), lambda i, ps: (i, 0)),
            pl.BlockSpec(seg.mod_block(), lambda i, ps: seg.mod_index(i)),
            pl.BlockSpec((1, D_MODEL), row),
            pl.BlockSpec((1, D_MODEL), row),
        ],
        out_specs=pl.BlockSpec((seg.tm, D_MODEL), lambda i, ps: (i, 0)),
        scratch_shapes=[pltpu.VMEM((GATHER_AHEAD + 1, TOP_K, seg.tm, D_MODEL), F32),
                        pltpu.SemaphoreType.DMA((GATHER_AHEAD + 1,))],
    )
    return pl.pallas_call(
        kern,
        out_shape=jax.ShapeDtypeStruct((seg.rows, D_MODEL), F32),
        grid_spec=grid_spec,
        compiler_params=_params("arbitrary"),
        name="moe_combine_postnorm",
    )(pos, y_sorted, route, x, gate, ln_g, ln_b)


def _rope_tables(pos):
    half = HEAD_DIM // 2
    inv = ROPE_THETA ** (-jnp.arange(half, dtype=F32) / half)
    ang = pos.astype(F32)[:, None] * inv[None, :]
    cos = jnp.cos(ang)
    sin = jnp.sin(ang)
    return jnp.concatenate([cos, cos], axis=-1), jnp.concatenate([-sin, sin], axis=-1)


def kernel(x_prompt, x_sample, cache_k, cache_v, state_conv, page_table, c_prompt, c_sample, ada_w, ada_b, ln_g, ln_b, conv_w_in, conv_b_in, conv_dw, conv_dw_b, conv_ln_g, conv_ln_b, conv_w_out, conv_b_out, attn_w_qkv, attn_lambda, attn_subln_g, attn_w_out, moe_w_group, moe_b_group, moe_w_expert, moe_b_expert, moe_w_in, moe_w_out):
    bp, tp, d = x_prompt.shape
    bs, ts, _ = x_sample.shape
    n_pages = page_table.shape[1]
    past_len = n_pages * PAGE_SIZE
    mp, ms = bp * tp, bs * ts
    assert d == D_MODEL and tp % 512 == 0 and ms % 8 == 0

    seg_p = Seg(mp, 512, tp, per_row=False)
    seg_p256 = Seg(mp, 256, tp, per_row=False)
    seg_s = Seg(ms, ms, ts, per_row=True)

    n_c = bp + bs
    c_rows = -(-n_c // 8) * 8
    c_pad = jnp.concatenate([c_prompt, c_sample, jnp.zeros((c_rows - n_c, d), F32)], axis=0)
    mods = ada_all(c_pad, ada_w.reshape(DEPTH * 2, d, 3 * d), ada_b.reshape(DEPTH * 2, 1, 3 * d))

    def mod_params(i, sub):
        m = mods[i * 2 + sub]
        out_p = [m[:bp, c * d:(c + 1) * d].reshape(bp, 1, d) for c in range(3)]
        out_s = [jnp.repeat(m[bp:n_c, c * d:(c + 1) * d], ts, axis=0).reshape(1, ms, d) for c in range(3)]
        return out_p, out_s

    cos_p, sin_p = _rope_tables(jnp.arange(tp, dtype=jnp.int32))
    cos_s, sin_s = _rope_tables(jnp.tile(past_len + jnp.arange(ts, dtype=jnp.int32), bs))

    x_p = x_prompt.reshape(mp, d)
    x_s = x_sample.reshape(ms, d)
    n_attn, pool = cache_k.shape[:2]
    n_hj = 2 * N_HEADS
    cache_k2 = cache_k.reshape(n_attn, pool, PAGE_SIZE * n_hj, HEAD_DIM)
    cache_v2 = cache_v.reshape(n_attn, pool, PAGE_SIZE * N_HEADS, V_DIM)

    n_assign = (mp + ms) * TOP_K
    n_tiles = -(-n_assign // MOE_TM) + N_EXPERTS - 1

    k_p = k_s = None
    v_p_rows, v_s_rows, conv_p_rows, conv_s_rows = [], [], [], []
    row2 = lambda a: a.reshape(1, -1)

    for i in range(DEPTH):
        l = i // N_MIXERS
        (sh_p, sc_p, gt_p), (sh_s, sc_s, gt_s) = mod_params(i, 0)
        lg, lb = row2(ln_g[i, 0]), row2(ln_b[i, 0])
        if i % N_MIXERS == 0:
            w_in = conv_w_in[l].astype(BF16)
            b_in = row2(conv_b_in[l])
            w_out = conv_w_out[l].astype(BF16)
            b_out = row2(conv_b_out[l])
            dw = jnp.concatenate([conv_dw[l], jnp.zeros((HALO - CONV_WIDTH, d), F32)], axis=0)
            cw = (dw, row2(conv_dw_b[l]), row2(conv_ln_g[l]), row2(conv_ln_b[l]))

            glu_p = mod_mm_glu(seg_p, x_p, sh_p, sc_p, w_in, b_in).reshape(bp, tp, d)
            tt = 256
            halo_idx = lambda b, t: (b, jnp.maximum(t * (tt // HALO) - 1, 0), 0)
            y_p = dwconv_ln_silu(glu_p, glu_p, halo_idx, True, tt, *cw).reshape(mp, d)
            conv_p_rows.append(glu_p[:, tp - (CONV_WIDTH - 1):])
            x_p = mm_postnorm(seg_p, y_p, w_out, b_out, x_p, gt_p, lg, lb)

            glu_s = mod_mm_glu(seg_s, x_s, sh_s, sc_s, w_in, b_in).reshape(bs, ts, d)
            hist = state_conv[l].astype(F32)
            halo_s = jnp.concatenate([jnp.zeros((bs, HALO - (CONV_WIDTH - 1), d), F32), hist], axis=1)
            cur_s = jnp.concatenate([glu_s, jnp.zeros((bs, 8 - ts, d), F32)], axis=1)
            y_s = dwconv_ln_silu(cur_s, halo_s, lambda b, t: (b, 0, 0), False, 8, *cw)[:, :ts].reshape(ms, d)
            conv_s_rows.append(jnp.concatenate([hist, glu_s], axis=1)[:, ts:])
            x_s = mm_postnorm(seg_s, y_s, w_out, b_out, x_s, gt_s, lg, lb)
        else:
            lam_init = 0.8 - 0.6 * math.exp(-0.3 * i)
            w_qkv = attn_w_qkv[l].astype(BF16)
            w_out = attn_w_out[l].astype(BF16)
            no_bias = jnp.zeros((1, d), F32)
            lam_vecs = attn_lambda[l].astype(F32)
            sub_g = row2(attn_subln_g[l])

            k_p, v_p, qkv_p_bf = mod_mm_qkv(seg_p, l, n_attn, x_p, sh_p, sc_p, w_qkv, cos_p, sin_p, prev_k=k_p)
            v_p_rows.append(v_p.reshape(bp, tp, N_HEADS, V_DIM))
            o_p = flash_diff_attention(qkv_p_bf, bp, tp, lam_vecs, sub_g, lam_init)
            x_p = mm_postnorm(seg_p, o_p, w_out, no_bias, x_p, gt_p, lg, lb)

            k_s, v_s, qkv_s_bf = mod_mm_qkv(seg_s, l, n_attn, x_s, sh_s, sc_s, w_qkv, cos_s, sin_s, prev_k=k_s)
            v_s_rows.append(v_s.reshape(bs, ts, N_HEADS, V_DIM))
            q_rows = qkv_s_bf[0].astype(F32).reshape(bs, ts, n_hj, HEAD_DIM).transpose(0, 2, 1, 3)
            q_rows = jnp.pad(q_rows, ((0, 0), (0, 0), (0, Q_ROWS - ts), (0, 0))).reshape(bs, n_hj * Q_ROWS, HEAD_DIM)
            k_new = jnp.pad(k_s[l].reshape(bs, ts * n_hj, HEAD_DIM), ((0, 0), (0, (PAGE_SIZE - ts) * n_hj), (0, 0)))
            v_new = jnp.pad(v_s.reshape(bs, ts * N_HEADS, V_DIM), ((0, 0), (0, (PAGE_SIZE - ts) * N_HEADS), (0, 0)))
            o_s = paged_diff_attention(l, page_table, q_rows, cache_k2, cache_v2, k_new, v_new, lam_vecs, sub_g, lam_init)
            o_s = o_s.reshape(bs, N_HEADS, Q_ROWS, V_DIM)[:, :, :ts].transpose(0, 2, 1, 3).reshape(ms, d)
            x_s = mm_postnorm(seg_s, o_s, w_out, no_bias, x_s, gt_s, lg, lb)

        (sh_p, sc_p, gt_p), (sh_s, sc_s, gt_s) = mod_params(i, 1)
        lg, lb = row2(ln_g[i, 1]), row2(ln_b[i, 1])
        n_logit = N_GROUPS + N_EXPERTS
        w_route = jnp.concatenate([moe_w_group[i], moe_w_expert[i], jnp.zeros((d, ROUTE_LANES - n_logit), F32)], axis=1)
        b_route = jnp.concatenate([moe_b_group[i], moe_b_expert[i], jnp.zeros((ROUTE_LANES - n_logit,), F32)]).reshape(1, -1)
        routed = moe_router(seg_p, mp + ms, 0, x_p, sh_p, sc_p, w_route, b_route)
        h_all, route = moe_router(seg_s, mp + ms, mp, x_s, sh_s, sc_s, w_route, b_route, prev=routed)
        ids = route[:, :TOP_K].astype(jnp.int32)
        pos, src_rows, tiles = route_layout(ids, n_tiles)
        y_sorted = moe_experts(i, tiles, src_rows, h_all, moe_w_in, moe_w_out)
        x_p = moe_combine_postnorm(seg_p256, pos[:mp * TOP_K], y_sorted, route[:mp], x_p, gt_p, lg, lb)
        x_s = moe_combine_postnorm(seg_s, pos[mp * TOP_K:], y_sorted, route[mp:], x_s, gt_s, lg, lb)

    return (x_p.reshape(bp, tp, d), x_s.reshape(bs, ts, d),
            k_p.reshape(n_attn, bp, tp, N_HEADS, 2, HEAD_DIM), jnp.stack(v_p_rows), jnp.stack(conv_p_rows),
            k_s.reshape(n_attn, bs, ts, N_HEADS, 2, HEAD_DIM), jnp.stack(v_s_rows), jnp.stack(conv_s_rows))
```

```python
import functools
import math

import jax
import jax.numpy as jnp
from jax import lax
from jax.experimental import pallas as pl
from jax.experimental.pallas import tpu as pltpu

D_MODEL = 2048
DEPTH = 4
PAGE_SIZE = 128
N_MIXERS = 2
CONV_WIDTH = 31
N_HEADS = 8
HEAD_DIM = D_MODEL // (2 * N_HEADS)
V_DIM = 2 * HEAD_DIM
ROPE_THETA = 10000.0
N_GROUPS = 4
EXPERTS_PER_GROUP = 8
N_EXPERTS = N_GROUPS * EXPERTS_PER_GROUP
TOP_K = 2
D_EXPERT = D_MODEL // 4
ALPHA = (2 * DEPTH) ** 0.25
LN_EPS = 1e-5

F32 = jnp.float32
BF16 = jnp.bfloat16
LANES = 128
HALO = 32
VMEM_LIMIT = 56 * 1024 * 1024
MOE_TM = 256
ROUTE_LANES = 128
NEG_INF = float("-inf")
Q_SCALE = HEAD_DIM ** -0.5 * math.log2(math.e)
DEC_PAGES = 8
Q_ROWS = 8
GATHER_AHEAD = 2


def _params(*sem):
    return pltpu.CompilerParams(dimension_semantics=sem, vmem_limit_bytes=VMEM_LIMIT)


def _dot(a, b):
    return jnp.dot(a, b, preferred_element_type=F32)


def _dot_nt(a, b):
    return lax.dot_general(a, b, (((1,), (1,)), ((), ())), preferred_element_type=F32)


def _layer_norm(z, g, b):
    mu = jnp.mean(z, axis=-1, keepdims=True)
    zc = z - mu
    var = jnp.mean(zc * zc, axis=-1, keepdims=True)
    return zc * lax.rsqrt(var + LN_EPS) * g + b


class Seg:
    def __init__(self, rows, tm, rows_per_batch, per_row):
        self.rows = rows
        self.tm = tm
        self.nt = rows // tm
        self.per_row = per_row
        self.tiles_per_batch = max(rows_per_batch // tm, 1)

    def mod_block(self):
        return (1, self.tm if self.per_row else 1, D_MODEL)

    def mod_index(self, i):
        return (0 if self.per_row else i // self.tiles_per_batch, 0, 0)

    def pos_index(self, i):
        return (0 if self.per_row else i % self.tiles_per_batch, 0)


def _ada_kernel(c_ref, w_ref, b_ref, o_ref):
    o_ref[0] = _dot(c_ref[...].astype(BF16), w_ref[0].astype(BF16)) + b_ref[0]


def ada_all(c_pad, w, b):
    n_sub, _, n_out = w.shape
    rows = c_pad.shape[0]
    tn = 1024
    return pl.pallas_call(
        _ada_kernel,
        out_shape=jax.ShapeDtypeStruct((n_sub, rows, n_out), F32),
        grid=(n_sub, n_out // tn),
        in_specs=[
            pl.BlockSpec((rows, D_MODEL), lambda s, j: (0, 0)),
            pl.BlockSpec((1, D_MODEL, tn), lambda s, j: (s, 0, j)),
            pl.BlockSpec((1, 1, tn), lambda s, j: (s, 0, j)),
        ],
        out_specs=pl.BlockSpec((1, rows, tn), lambda s, j: (s, 0, j)),
        compiler_params=_params("arbitrary", "arbitrary"),
        name="ada_all",
    )(c_pad, w, b)


def _glu_kernel(x_ref, sh_ref, sc_ref, wa_ref, wg_ref, ba_ref, bg_ref, o_ref, h_scr):
    @pl.when(pl.program_id(1) == 0)
    def _():
        h_scr[...] = (x_ref[...] * (1.0 + sc_ref[0]) + sh_ref[0]).astype(BF16)

    h = h_scr[...]
    a = _dot(h, wa_ref[...]) + ba_ref[...]
    g = _dot(h, wg_ref[...]) + bg_ref[...]
    o_ref[...] = a * jax.nn.sigmoid(g)


def mod_mm_glu(seg, x, shift, scale, w_in, b_in):
    tn = 512
    nj = D_MODEL // tn
    return pl.pallas_call(
        _glu_kernel,
        out_shape=jax.ShapeDtypeStruct((seg.rows, D_MODEL), F32),
        grid=(seg.nt, nj),
        in_specs=[
            pl.BlockSpec((seg.tm, D_MODEL), lambda i, j: (i, 0)),
            pl.BlockSpec(seg.mod_block(), lambda i, j: seg.mod_index(i)),
            pl.BlockSpec(seg.mod_block(), lambda i, j: seg.mod_index(i)),
            pl.BlockSpec((D_MODEL, tn), lambda i, j: (0, j)),
            pl.BlockSpec((D_MODEL, tn), lambda i, j: (0, j + nj)),
            pl.BlockSpec((1, tn), lambda i, j: (0, j)),
            pl.BlockSpec((1, tn), lambda i, j: (0, j + nj)),
        ],
        out_specs=pl.BlockSpec((seg.tm, tn), lambda i, j: (i, j)),
        scratch_shapes=[pltpu.VMEM((seg.tm, D_MODEL), BF16)],
        compiler_params=_params("arbitrary", "arbitrary"),
        name="mod_mm_glu",
    )(x, shift, scale, w_in, w_in, b_in, b_in)


def _dwconv_kernel(cur_ref, halo_ref, dw_ref, dwb_ref, g_ref, b_ref, o_ref, xs_scr, sh_scr, y_scr, *, tt, zero_first):
    hist = halo_ref[0]
    if zero_first:
        hist = jnp.where(pl.program_id(1) == 0, jnp.zeros_like(hist), hist)
    xs_scr[0:HALO, :] = hist
    xs_scr[HALO:HALO + tt, :] = cur_ref[0]

    rch = min(tt, 32)
    cch = sh_scr.shape[2]
    span = sh_scr.shape[1]
    lead = HALO - (CONV_WIDTH - 1)

    def col_body(cc, carry):
        cols = pl.ds(pl.multiple_of(cc * cch, cch), cch)
        for r in range(1, 8):
            sh_scr[r - 1] = xs_scr[r:r + span, cols]
        for rc in range(tt // rch):
            acc = jnp.zeros((rch, cch), F32)
            for j in range(CONV_WIDTH):
                q, r = divmod(lead + j, 8)
                u0 = rc * rch + 8 * q
                win = xs_scr[u0:u0 + rch, cols] if r == 0 else sh_scr[r - 1, u0:u0 + rch, :]
                acc = acc + dw_ref[j:j + 1, cols] * win
            y_scr[rc * rch:(rc + 1) * rch, cols] = acc + dwb_ref[:, cols]
        return carry

    lax.fori_loop(0, D_MODEL // cch, col_body, 0)
    y = _layer_norm(y_scr[...], g_ref[...], b_ref[...])
    o_ref[0] = (y * jax.nn.sigmoid(y)).astype(BF16)


def dwconv_ln_silu(cur, halo, halo_index, zero_first, tt, dw, dw_b, ln_g, ln_b):
    nb, t, _ = cur.shape
    kern = functools.partial(_dwconv_kernel, tt=tt, zero_first=zero_first)
    return pl.pallas_call(
        kern,
        out_shape=jax.ShapeDtypeStruct((nb, t, D_MODEL), BF16),
        grid=(nb, t // tt),
        in_specs=[
            pl.BlockSpec((1, tt, D_MODEL), lambda b, i: (b, i, 0)),
            pl.BlockSpec((1, HALO, D_MODEL), halo_index),
            pl.BlockSpec((HALO, D_MODEL), lambda b, i: (0, 0)),
            pl.BlockSpec((1, D_MODEL), lambda b, i: (0, 0)),
            pl.BlockSpec((1, D_MODEL), lambda b, i: (0, 0)),
            pl.BlockSpec((1, D_MODEL), lambda b, i: (0, 0)),
        ],
        out_specs=pl.BlockSpec((1, tt, D_MODEL), lambda b, i: (b, i, 0)),
        scratch_shapes=[pltpu.VMEM((HALO + tt, D_MODEL), F32),
                        pltpu.VMEM((7, tt + HALO - 8, 256), F32),
                        pltpu.VMEM((tt, D_MODEL), F32)],
        compiler_params=_params("arbitrary", "arbitrary"),
        name="dwconv_ln_silu",
    )(cur, halo, dw, dw_b, ln_g, ln_b)


def _mm_postnorm_kernel(y_ref, w_ref, bias_ref, x_ref, gate_ref, g_ref, b_ref, o_ref):
    tm = y_ref.shape[0]
    rc = min(tm, 128)
    gate = gate_ref[0]
    for r in range(tm // rc):
        rows = slice(r * rc, (r + 1) * rc)
        out = _dot(y_ref[rows, :], w_ref[...]) + bias_ref[...]
        z = ALPHA * x_ref[rows, :] + (gate if gate.shape[0] == 1 else gate[rows]) * out
        o_ref[rows, :] = _layer_norm(z, g_ref[...], b_ref[...])


def mm_postnorm(seg, y, w, bias, x, gate, ln_g, ln_b):
    k = y.shape[1]
    row = lambda i: (0, 0)
    return pl.pallas_call(
        _mm_postnorm_kernel,
        out_shape=jax.ShapeDtypeStruct((seg.rows, D_MODEL), F32),
        grid=(seg.nt,),
        in_specs=[
            pl.BlockSpec((seg.tm, k), lambda i: (i, 0)),
            pl.BlockSpec((k, D_MODEL), row),
            pl.BlockSpec((1, D_MODEL), row),
            pl.BlockSpec((seg.tm, D_MODEL), lambda i: (i, 0)),
            pl.BlockSpec(seg.mod_block(), seg.mod_index),
            pl.BlockSpec((1, D_MODEL), row),
            pl.BlockSpec((1, D_MODEL), row),
        ],
        out_specs=pl.BlockSpec((seg.tm, D_MODEL), lambda i: (i, 0)),
        compiler_params=_params("arbitrary"),
        name="mm_postnorm",
    )(y, w, bias, x, gate, ln_g, ln_b)


def _qkv_kernel(x_ref, sh_ref, sc_ref, w_ref, cos_ref, sin_ref, *rest, tm, n_fill):
    ok_ref, ov_ref, ob_ref, h_scr = rest[-4:]
    part = pl.program_id(1) - n_fill
    cw = 4 * HEAD_DIM
    n_hj = D_MODEL // HEAD_DIM

    def rope_chunks(store_rot):
        h = h_scr[...]
        cos = cos_ref[...]
        sin = sin_ref[...]
        for c in range(D_MODEL // cw):
            acc = _dot(h, w_ref[:, c * cw:(c + 1) * cw])
            for g in range(cw // HEAD_DIM):
                blk = acc[:, g * HEAD_DIM:(g + 1) * HEAD_DIM]
                rot = blk * cos + pltpu.roll(blk, HEAD_DIM // 2, 1) * sin
                store_rot(c * (cw // HEAD_DIM) + g, rot)

    @pl.when(part == 0)
    def _():
        h_scr[...] = (x_ref[...] * (1.0 + sc_ref[0]) + sh_ref[0]).astype(BF16)

        def store_q(hj, rot):
            ob_ref[0, :, hj * HEAD_DIM:(hj + 1) * HEAD_DIM] = (rot * Q_SCALE).astype(BF16)
        rope_chunks(store_q)

    @pl.when(part == 1)
    def _():
        def store_k(hj, rot):
            ob_ref[0, :, hj * HEAD_DIM:(hj + 1) * HEAD_DIM] = rot.astype(BF16)
            ok_ref[0, pl.ds(hj, tm, stride=n_hj), :] = rot
        rope_chunks(store_k)

    @pl.when(part == 2)
    def _():
        h = h_scr[...]
        for c in range(D_MODEL // cw):
            acc = _dot(h, w_ref[:, c * cw:(c + 1) * cw])
            ob_ref[0, :, c * cw:(c + 1) * cw] = acc.astype(BF16)
            ov_ref[:, c * cw:(c + 1) * cw] = acc

    @pl.when(part < 0)
    def _():
        ok_ref[...] = jnp.zeros_like(ok_ref)


def mod_mm_qkv(seg, layer, n_layers, x, shift, scale, w_qkv, cos, sin, prev_k=None):
    n_hj = D_MODEL // HEAD_DIM
    aliased = prev_k is not None
    n_fill = 0 if aliased else n_layers - 1
    part = lambda p: jnp.clip(p - n_fill, 0, 2)

    def k_slab(p):
        return jnp.where(p >= n_fill, layer, jnp.where(p < layer, p, p + 1))

    in_specs = [
        pl.BlockSpec((seg.tm, D_MODEL), lambda i, p: (i, 0)),
        pl.BlockSpec(seg.mod_block(), lambda i, p: seg.mod_index(i)),
        pl.BlockSpec(seg.mod_block(), lambda i, p: seg.mod_index(i)),
        pl.BlockSpec((D_MODEL, D_MODEL), lambda i, p: (0, part(p))),
        pl.BlockSpec((seg.tm, HEAD_DIM), lambda i, p: seg.pos_index(i)),
        pl.BlockSpec((seg.tm, HEAD_DIM), lambda i, p: seg.pos_index(i)),
    ]
    args = [x, shift, scale, w_qkv, cos, sin]
    aliases = {}
    if aliased:
        in_specs.append(pl.BlockSpec(memory_space=pl.ANY))
        aliases = {len(args): 0}
        args.append(prev_k)
    return pl.pallas_call(
        functools.partial(_qkv_kernel, tm=seg.tm, n_fill=n_fill),
        out_shape=(jax.ShapeDtypeStruct((n_layers, seg.rows * n_hj, HEAD_DIM), F32),
                   jax.ShapeDtypeStruct((seg.rows, D_MODEL), F32),
                   jax.ShapeDtypeStruct((3, seg.rows, D_MODEL), BF16)),
        grid=(seg.nt, 3 + n_fill),
        in_specs=in_specs,
        out_specs=(pl.BlockSpec((1, seg.tm * n_hj, HEAD_DIM), lambda i, p: (k_slab(p), i, 0)),
                   pl.BlockSpec((seg.tm, D_MODEL), lambda i, p: (i, 0)),
                   pl.BlockSpec((1, seg.tm, D_MODEL), lambda i, p: (part(p), i, 0))),
        scratch_shapes=[pltpu.VMEM((seg.tm, D_MODEL), BF16)],
        input_output_aliases=aliases,
        compiler_params=_params("arbitrary", "arbitrary"),
        name="mod_mm_qkv",
    )(*args)


def _diff_lambda(lam_ref, lam_init):
    lv = lam_ref[...]
    s01 = jnp.sum(lv[0:1] * lv[1:2], axis=-1, keepdims=True)
    s23 = jnp.sum(lv[2:3] * lv[3:4], axis=-1, keepdims=True)
    return jnp.exp(s01) - jnp.exp(s23) + lam_init


def _head_rms(o, g, lam_init):
    ms = jnp.mean(o * o, axis=-1, keepdims=True)
    return o * lax.rsqrt(ms + LN_EPS) * g * (1.0 - lam_init)


def _flash_kernel(qi_ref, ki_ref, q_ref, k_ref, v_ref, lam_ref, g_ref, o_ref, m_scr, l_scr, acc_scr, s_scr, p_scr, a_scr,
                  *, tq, lam_init):
    pr = pl.program_id(2)
    qi = qi_ref[pr]
    ki = ki_ref[pr]
    rc = 32
    n_kt = tq // LANES

    @pl.when(ki == 0)
    def _():
        m_scr[...] = jnp.full_like(m_scr, NEG_INF)
        l_scr[...] = jnp.zeros_like(l_scr)
        acc_scr[...] = jnp.zeros_like(acc_scr)

    def step(masked):
        for j in range(2):
            lanes = slice(j * HEAD_DIM, (j + 1) * HEAD_DIM)
            s_scr[j] = _dot_nt(q_ref[0, :, lanes], k_ref[0, :, lanes])
        for j in range(2):
            def chunk(c):
                r0 = c * rc
                rows = slice(r0, r0 + rc)
                parts = [s_scr[j, rows, kt * LANES:(kt + 1) * LANES] for kt in range(n_kt)]
                if masked:
                    row = r0 + lax.broadcasted_iota(jnp.int32, (rc, LANES), 0)
                    col = lax.broadcasted_iota(jnp.int32, (rc, LANES), 1)
                    parts = [jnp.where(col + kt * LANES <= row, parts[kt], NEG_INF) for kt in range(n_kt)]
                mx = parts[0]
                for kt in range(1, n_kt):
                    mx = jnp.maximum(mx, parts[kt])
                m_old = m_scr[j, rows, :]
                m_new = jnp.maximum(m_old, jnp.max(mx, axis=1, keepdims=True))
                alpha = jnp.exp2(m_old - m_new)
                lsum = jnp.zeros((rc, LANES), F32)
                for kt in range(n_kt):
                    p = jnp.exp2(parts[kt] - m_new)
                    lsum = lsum + p
                    p_scr[j, rows, kt * LANES:(kt + 1) * LANES] = p.astype(BF16)
                l_scr[j, rows, :] = alpha * l_scr[j, rows, :] + lsum
                m_scr[j, rows, :] = m_new
                a_scr[j, rows, :] = alpha

            for c in range(tq // rc):
                chunk(c)
            alpha = a_scr[j]
            acc_scr[j] = jnp.concatenate([alpha, alpha], axis=1) * acc_scr[j] + _dot(p_scr[j], v_ref[0])

    @pl.when(ki < qi)
    def _():
        step(False)

    @pl.when(ki == qi)
    def _():
        step(True)
        lam = _diff_lambda(lam_ref, lam_init)
        l0 = jnp.sum(l_scr[0], axis=1, keepdims=True)
        l1 = jnp.sum(l_scr[1], axis=1, keepdims=True)
        o = acc_scr[0] / l0 - lam * (acc_scr[1] / l1)
        o_ref[...] = _head_rms(o, g_ref[...], lam_init).astype(BF16)


def flash_diff_attention(qkv_bf, nb, t, lam_vecs, subln_g, lam_init):
    tq = 512
    nq = t // tq
    pairs = [(a, b) for a in range(nq) for b in range(a + 1)]
    qi_tab = jnp.asarray([p[0] for p in pairs], jnp.int32)
    ki_tab = jnp.asarray([p[1] for p in pairs], jnp.int32)
    kern = functools.partial(_flash_kernel, tq=tq, lam_init=lam_init)
    grid_spec = pltpu.PrefetchScalarGridSpec(
        num_scalar_prefetch=2,
        grid=(nb, N_HEADS, len(pairs)),
        in_specs=[
            pl.BlockSpec((1, tq, V_DIM), lambda b, h, p, qt, kt: (0, b * nq + qt[p], h)),
            pl.BlockSpec((1, tq, V_DIM), lambda b, h, p, qt, kt: (1, b * nq + kt[p], h)),
            pl.BlockSpec((1, tq, V_DIM), lambda b, h, p, qt, kt: (2, b * nq + kt[p], h)),
            pl.BlockSpec((4, HEAD_DIM), lambda b, h, p, qt, kt: (0, 0)),
            pl.BlockSpec((1, V_DIM), lambda b, h, p, qt, kt: (0, 0)),
        ],
        out_specs=pl.BlockSpec((tq, V_DIM), lambda b, h, p, qt, kt: (b * nq + qt[p], h)),
        scratch_shapes=[
            pltpu.VMEM((2, tq, LANES), F32),
            pltpu.VMEM((2, tq, LANES), F32),
            pltpu.VMEM((2, tq, V_DIM), F32),
            pltpu.VMEM((2, tq, tq), F32),
            pltpu.VMEM((2, tq, tq), BF16),
            pltpu.VMEM((2, tq, LANES), F32),
        ],
    )
    return pl.pallas_call(
        kern,
        out_shape=jax.ShapeDtypeStruct((nb * t, D_MODEL), BF16),
        grid_spec=grid_spec,
        compiler_params=_params("arbitrary", "arbitrary", "arbitrary"),
        name="flash_diff_attention",
    )(qi_tab, ki_tab, qkv_bf, qkv_bf, qkv_bf, lam_vecs, subln_g)


def _decode_kernel(pt_ref, q_ref, *refs, n_steps, lam_init):
    k_refs = refs[:DEC_PAGES]
    v_refs = refs[DEC_PAGES:2 * DEC_PAGES]
    kn_ref, vn_ref, spread_ref, lam_ref, g_ref, o_ref, m_scr, l_scr, acc_scr, own_scr = refs[2 * DEC_PAGES:]
    p = pl.program_id(1)
    n_hj = 2 * N_HEADS
    grp = 2 * Q_ROWS

    @pl.when(p == 0)
    def _():
        m_scr[...] = jnp.full_like(m_scr, NEG_INF)
        l_scr[...] = jnp.zeros_like(l_scr)
        acc_scr[...] = jnp.zeros_like(acc_scr)
        row_head = lax.broadcasted_iota(jnp.int32, own_scr.shape, 0) // grp
        col_head = lax.broadcasted_iota(jnp.int32, own_scr.shape, 1) % N_HEADS
        own_scr[...] = (row_head == col_head).astype(F32)

    def process(k_pages, v_pages, keep):
        q_parts = [q_ref[0, hj * Q_ROWS:(hj + 1) * Q_ROWS, :].astype(BF16) for hj in range(n_hj)]
        s_pages = []
        for k_ref in k_pages:
            parts = []
            for hj in range(n_hj):
                k_hj = k_ref[pl.ds(hj, PAGE_SIZE, stride=n_hj), :].astype(BF16)
                parts.append(_dot_nt(q_parts[hj], k_hj))
            s_pages.append(jnp.concatenate(parts, axis=0))
        s = jnp.concatenate(s_pages, axis=1)
        if keep is not None:
            s = jnp.where(keep, s, NEG_INF)
        m_old = m_scr[...]
        m_new = jnp.maximum(m_old, jnp.max(s, axis=1, keepdims=True))
        alpha = jnp.exp2(m_old - m_new)
        pe = jnp.exp2(s - m_new)
        l_scr[...] = alpha * l_scr[...] + jnp.sum(pe, axis=1, keepdims=True)
        m_scr[...] = m_new
        own = own_scr[...]
        pv = None
        for c, v_ref in enumerate(v_pages):
            pe_c = pe[:, c * PAGE_SIZE:(c + 1) * PAGE_SIZE].astype(BF16)
            spread = _dot(pe_c, spread_ref[...]) * own
            term = _dot(spread.astype(BF16), v_ref[...].astype(BF16))
            pv = term if pv is None else pv + term
        acc_scr[...] = alpha * acc_scr[...] + pv

    process([r.at[0, 0] for r in k_refs], [r.at[0, 0] for r in v_refs], None)

    @pl.when(p == n_steps - 1)
    def _():
        row = lax.broadcasted_iota(jnp.int32, (n_hj * Q_ROWS, PAGE_SIZE), 0)
        col = lax.broadcasted_iota(jnp.int32, (n_hj * Q_ROWS, PAGE_SIZE), 1)
        process([kn_ref.at[0]], [vn_ref.at[0]], col <= row % Q_ROWS)

        lam = _diff_lambda(lam_ref, lam_init)
        an = acc_scr[...] / l_scr[...]
        outs = []
        for h in range(N_HEADS):
            outs.append(an[h * grp:h * grp + Q_ROWS] - lam * an[h * grp + Q_ROWS:(h + 1) * grp])
        o = jnp.concatenate(outs, axis=0)
        o_ref[0] = _head_rms(o, g_ref[...], lam_init).astype(BF16)


def paged_diff_attention(layer, page_table, q_rows, cache_k, cache_v, k_new, v_new, lam_vecs, subln_g, lam_init):
    nb, n_pages = page_table.shape
    assert n_pages % DEC_PAGES == 0
    n_steps = n_pages // DEC_PAGES
    n_hj = 2 * N_HEADS
    kern = functools.partial(_decode_kernel, n_steps=n_steps, lam_init=lam_init)

    def page_idx(c):
        return lambda b, p, pt: (layer, pt[b * n_pages + p * DEC_PAGES + c], 0, 0)

    fixed = lambda b, p, pt: (b, 0, 0)
    grid_spec = pltpu.PrefetchScalarGridSpec(
        num_scalar_prefetch=1,
        grid=(nb, n_steps),
        in_specs=(
            [pl.BlockSpec((1, n_hj * Q_ROWS, HEAD_DIM), fixed)]
            + [pl.BlockSpec((1, 1, PAGE_SIZE * n_hj, HEAD_DIM), page_idx(c)) for c in range(DEC_PAGES)]
            + [pl.BlockSpec((1, 1, PAGE_SIZE * N_HEADS, V_DIM), page_idx(c)) for c in range(DEC_PAGES)]
            + [pl.BlockSpec((1, PAGE_SIZE * n_hj, HEAD_DIM), fixed),
               pl.BlockSpec((1, PAGE_SIZE * N_HEADS, V_DIM), fixed),
               pl.BlockSpec((PAGE_SIZE, PAGE_SIZE * N_HEADS), lambda b, p, pt: (0, 0)),
               pl.BlockSpec((4, HEAD_DIM), lambda b, p, pt: (0, 0)),
               pl.BlockSpec((1, V_DIM), lambda b, p, pt: (0, 0))]),
        out_specs=pl.BlockSpec((1, N_HEADS * Q_ROWS, V_DIM), fixed),
        scratch_shapes=[
            pltpu.VMEM((n_hj * Q_ROWS, 1), F32),
            pltpu.VMEM((n_hj * Q_ROWS, 1), F32),
            pltpu.VMEM((n_hj * Q_ROWS, V_DIM), F32),
            pltpu.VMEM((n_hj * Q_ROWS, PAGE_SIZE * N_HEADS), F32),
        ],
    )
    spread = jnp.repeat(jnp.eye(PAGE_SIZE, dtype=BF16), N_HEADS, axis=1)
    return pl.pallas_call(
        kern,
        out_shape=jax.ShapeDtypeStruct((nb, N_HEADS * Q_ROWS, V_DIM), BF16),
        grid_spec=grid_spec,
        compiler_params=_params("arbitrary", "arbitrary"),
        name="paged_diff_attention",
    )(page_table.reshape(-1), q_rows, *([cache_k] * DEC_PAGES), *([cache_v] * DEC_PAGES), k_new, v_new,
      spread, lam_vecs, subln_g)


def _router_kernel(x_ref, sh_ref, sc_ref, w_ref, b_ref, *rest, n_real):
    h_ref, r_ref = rest[-2:]

    @pl.when(pl.program_id(0) >= n_real)
    def _():
        h_ref[...] = jnp.zeros_like(h_ref)
        r_ref[...] = jnp.zeros_like(r_ref)

    @pl.when(pl.program_id(0) < n_real)
    def _():
        _route_rows(x_ref, sh_ref, sc_ref, w_ref, b_ref, h_ref, r_ref)


def _route_rows(x_ref, sh_ref, sc_ref, w_ref, b_ref, h_ref, r_ref):
    h = x_ref[...] * (1.0 + sc_ref[0]) + sh_ref[0]
    hb = h.astype(BF16)
    bits = pltpu.bitcast(hb.astype(F32), jnp.uint32)
    half = D_MODEL // 2
    h_ref[...] = (bits[:, :half] >> 16) | (bits[:, half:] & jnp.uint32(0xFFFF0000))
    logits = _dot(hb, w_ref[...].astype(BF16)) + b_ref[...]

    lane = lax.broadcasted_iota(jnp.int32, logits.shape, 1).astype(F32)

    def first_max(vals):
        vmax = jnp.max(vals, axis=1, keepdims=True)
        idx = jnp.min(jnp.where(vals == vmax, lane, float(ROUTE_LANES)), axis=1, keepdims=True)
        return vmax, idx

    gl = jnp.where(lane < N_GROUPS, logits, NEG_INF)
    gmax, gsel = first_max(gl)
    pg_sel = 1.0 / jnp.sum(jnp.exp(gl - gmax), axis=1, keepdims=True)

    lo = N_GROUPS + EXPERTS_PER_GROUP * gsel
    el = jnp.where((lane >= lo) & (lane < lo + EXPERTS_PER_GROUP), logits, NEG_INF)
    v0, i0 = first_max(el)
    v1, i1 = first_max(jnp.where(lane == i0, NEG_INF, el))
    e1 = jnp.exp(v1 - v0)
    w0 = pg_sel / (1.0 + e1)
    w1 = pg_sel * e1 / (1.0 + e1)

    out = jnp.where(lane == 0, i0 - N_GROUPS, 0.0)
    out = jnp.where(lane == 1, i1 - N_GROUPS, out)
    out = jnp.where(lane == 2, w0, out)
    out = jnp.where(lane == 3, w1, out)
    r_ref[...] = out


def moe_router(seg, rows_total, row_offset, x, shift, scale, w_route, b_route, prev=None):
    assert row_offset % seg.tm == 0
    blk0 = row_offset // seg.tm
    n_real = seg.nt
    n_fill = 0
    if prev is None and rows_total > row_offset + seg.rows:
        assert row_offset == 0 and rows_total - seg.rows <= seg.tm
        n_fill = 1
    real = lambda i: jnp.minimum(i, n_real - 1)
    row = lambda i: (0, 0)
    in_specs = [
        pl.BlockSpec((seg.tm, D_MODEL), lambda i: (real(i), 0)),
        pl.BlockSpec(seg.mod_block(), lambda i: seg.mod_index(real(i))),
        pl.BlockSpec(seg.mod_block(), lambda i: seg.mod_index(real(i))),
        pl.BlockSpec((D_MODEL, ROUTE_LANES), row),
        pl.BlockSpec((1, ROUTE_LANES), row),
    ]
    args = [x, shift, scale, w_route, b_route]
    aliases = {}
    if prev is not None:
        in_specs += [pl.BlockSpec(memory_space=pl.ANY), pl.BlockSpec(memory_space=pl.ANY)]
        aliases = {len(args): 0, len(args) + 1: 1}
        args += list(prev)
    return pl.pallas_call(
        functools.partial(_router_kernel, n_real=n_real),
        out_shape=(jax.ShapeDtypeStruct((rows_total, D_MODEL // 2), jnp.uint32),
                   jax.ShapeDtypeStruct((rows_total, ROUTE_LANES), F32)),
        grid=(n_real + n_fill,),
        in_specs=in_specs,
        out_specs=(pl.BlockSpec((seg.tm, D_MODEL // 2), lambda i: (blk0 + i, 0)),
                   pl.BlockSpec((seg.tm, ROUTE_LANES), lambda i: (blk0 + i, 0))),
        input_output_aliases=aliases,
        compiler_params=_params("arbitrary"),
        name="moe_router",
    )(*args)


def route_layout(expert_ids, n_tiles):
    n_assign = expert_ids.size
    e = expert_ids.reshape(n_assign)
    onehot = (e[:, None] == jnp.arange(N_EXPERTS, dtype=jnp.int32)[None, :]).astype(jnp.int32)
    csum = jnp.cumsum(onehot, axis=0)
    rank = jnp.sum(csum * onehot, axis=1) - 1
    counts = csum[-1]
    padded = ((counts + MOE_TM - 1) // MOE_TM) * MOE_TM
    ends = jnp.cumsum(padded)
    starts = ends - padded
    pos = starts[e] + rank
    src_rows = jnp.zeros(((n_tiles + GATHER_AHEAD) * MOE_TM,), jnp.int32).at[pos].set(
        jnp.arange(n_assign, dtype=jnp.int32) // TOP_K)
    tile_start = jnp.arange(n_tiles, dtype=jnp.int32) * MOE_TM
    valid = tile_start < ends[-1]
    tile_e = jnp.minimum(jnp.searchsorted(ends, tile_start, side="right"), N_EXPERTS - 1).astype(jnp.int32)
    last_e = jnp.max(jnp.where(valid, tile_e, 0))
    tile_e = jnp.where(valid, tile_e, last_e)
    prev_e = jnp.concatenate([jnp.full((1,), -1, jnp.int32), tile_e[:-1]])
    first = valid & (tile_e != prev_e)
    used = counts > 0
    used_experts = jnp.nonzero(used, size=N_EXPERTS, fill_value=0)[0].astype(jnp.int32)
    tile_k = (jnp.cumsum(used.astype(jnp.int32)) - 1)[tile_e]
    n_used = jnp.sum(used.astype(jnp.int32)).reshape(1)
    tiles = (valid.astype(jnp.int32), first.astype(jnp.int32), tile_k.astype(jnp.int32), used_experts, n_used)
    return pos.astype(jnp.int32), src_rows, tiles


def _moe_kernel(tv_ref, tf_ref, tk_ref, ek_ref, nk_ref, src_ref, h_hbm, win_hbm, wout_hbm, y_ref,
                buf, sem, wf_in, wf_out, wsem, win_b, wout_b, *, layer, n_tiles):
    i = pl.program_id(0)
    n_slots = buf.shape[0]
    slot = lax.rem(i, n_slots)
    n_used = nk_ref[0]

    def weight_copies(k, wslot):
        e = ek_ref[k]
        return (pltpu.make_async_copy(win_hbm.at[layer, e], wf_in.at[wslot], wsem.at[0, wslot]),
                pltpu.make_async_copy(wout_hbm.at[layer, e], wf_out.at[wslot], wsem.at[1, wslot]))

    def row_copy(tile, r, dst_slot):
        src = src_ref[tile * MOE_TM + r]
        return pltpu.make_async_copy(h_hbm.at[pl.ds(src, 1)], buf.at[dst_slot, pl.ds(r, 1)], sem.at[dst_slot])

    def wait_tile(dst_slot):
        pltpu.make_async_copy(h_hbm.at[pl.ds(0, MOE_TM)], buf.at[dst_slot], sem.at[dst_slot]).wait()

    def started_by_valid(tile):
        return tv_ref[jnp.maximum(tile - GATHER_AHEAD, 0)] == 1

    @pl.when(i == 0)
    def _():
        for t in range(GATHER_AHEAD):
            def body(r, carry):
                row_copy(t, r, t).start()
                return carry
            lax.fori_loop(0, MOE_TM, body, 0)
        for c in weight_copies(0, 0):
            c.start()

        @pl.when(n_used > 1)
        def _():
            for c in weight_copies(1, 1):
                c.start()

    @pl.when(tf_ref[i] == 1)
    def _():
        k = tk_ref[i]
        wslot = lax.rem(k, 2)
        for c in weight_copies(k, wslot):
            c.wait()
        win_b[...] = wf_in[wslot].astype(BF16)
        wout_b[...] = wf_out[wslot].astype(BF16)

        @pl.when(k + 2 < n_used)
        def _():
            for c in weight_copies(k + 2, wslot):
                c.start()

    @pl.when((i < GATHER_AHEAD) | started_by_valid(i))
    def _():
        wait_tile(slot)

    @pl.when(tv_ref[i] == 1)
    def _():
        words = buf[slot]
        x_lo = pltpu.bitcast(words << 16, F32).astype(BF16)
        x_hi = pltpu.bitcast(words & jnp.uint32(0xFFFF0000), F32).astype(BF16)
        nxt = lax.rem(i + GATHER_AHEAD, n_slots)
        for r in range(MOE_TM):
            row_copy(i + GATHER_AHEAD, r, nxt).start()
        half = D_MODEL // 2
        u = _dot(x_lo, win_b[:half, :]) + _dot(x_hi, win_b[half:, :])
        a = u[:, :D_EXPERT]
        act = a * jax.nn.sigmoid(a) * u[:, D_EXPERT:]
        y_ref[...] = _dot(act.astype(BF16), wout_b[...])

    @pl.when(tv_ref[i] == 0)
    def _():
        y_ref[...] = jnp.zeros_like(y_ref)

    @pl.when(i == n_tiles - 1)
    def _():
        for t in range(n_tiles, n_tiles + GATHER_AHEAD):
            @pl.when(started_by_valid(t))
            def _():
                wait_tile(t % n_slots)


def moe_experts(layer, tiles, src_rows, h_all, w_in, w_out):
    tile_valid, tile_first, tile_k, used_experts, n_used = tiles
    n_tiles = tile_valid.shape[0]
    kern = functools.partial(_moe_kernel, layer=layer, n_tiles=n_tiles)
    grid_spec = pltpu.PrefetchScalarGridSpec(
        num_scalar_prefetch=6,
        grid=(n_tiles,),
        in_specs=[pl.BlockSpec(memory_space=pl.ANY)] * 3,
        out_specs=pl.BlockSpec((MOE_TM, D_MODEL), lambda i, *_: (i, 0)),
        scratch_shapes=[
            pltpu.VMEM((GATHER_AHEAD + 1, MOE_TM, D_MODEL // 2), jnp.uint32),
            pltpu.SemaphoreType.DMA((GATHER_AHEAD + 1,)),
            pltpu.VMEM((2, D_MODEL, 2 * D_EXPERT), F32),
            pltpu.VMEM((2, D_EXPERT, D_MODEL), F32),
            pltpu.SemaphoreType.DMA((2, 2)),
            pltpu.VMEM((D_MODEL, 2 * D_EXPERT), BF16),
            pltpu.VMEM((D_EXPERT, D_MODEL), BF16),
        ],
    )
    return pl.pallas_call(
        kern,
        out_shape=jax.ShapeDtypeStruct((n_tiles * MOE_TM, D_MODEL), F32),
        grid_spec=grid_spec,
        compiler_params=_params("arbitrary"),
        name="moe_experts",
    )(tile_valid, tile_first, tile_k, used_experts, n_used, src_rows, h_all, w_in, w_out)


def _combine_kernel(pos_ref, y_hbm, r_ref, x_ref, gate_ref, g_ref, b_ref, o_ref, buf, sem, *, tm, nt):
    i = pl.program_id(0)
    n_slots = buf.shape[0]
    slot = lax.rem(i, n_slots)

    def row_copy(tile, r, k, dst_slot):
        src = pos_ref[(tile * tm + r) * TOP_K + k]
        return pltpu.make_async_copy(y_hbm.at[pl.ds(src, 1)], buf.at[dst_slot, k, pl.ds(r, 1)], sem.at[dst_slot])

    def wait_tile(dst_slot):
        for k in range(TOP_K):
            pltpu.make_async_copy(y_hbm.at[pl.ds(0, tm)], buf.at[dst_slot, k], sem.at[dst_slot]).wait()

    @pl.when(i == 0)
    def _():
        for t in range(GATHER_AHEAD):
            def body(r, carry):
                for k in range(TOP_K):
                    row_copy(t, r, k, t).start()
                return carry
            lax.fori_loop(0, tm, body, 0)

    wait_tile(slot)
    y0 = buf[slot, 0]
    y1 = buf[slot, 1]
    nxt = lax.rem(i + GATHER_AHEAD, n_slots)
    for r in range(tm):
        for k in range(TOP_K):
            row_copy(i + GATHER_AHEAD, r, k, nxt).start()
    route = r_ref[...]
    moe = route[:, 2:3] * y0 + route[:, 3:4] * y1
    z = ALPHA * x_ref[...] + gate_ref[0] * moe
    o_ref[...] = _layer_norm(z, g_ref[...], b_ref[...])

    @pl.when(i == nt - 1)
    def _():
        for t in range(nt, nt + GATHER_AHEAD):
            wait_tile(t % n_slots)


def moe_combine_postnorm(seg, pos, y_sorted, route, x, gate, ln_g, ln_b):
    kern = functools.partial(_combine_kernel, tm=seg.tm, nt=seg.nt)
    pos = jnp.concatenate([pos, jnp.zeros((GATHER_AHEAD * seg.tm * TOP_K,), jnp.int32)])
    row = lambda i, ps: (0, 0)
    grid_spec = pltpu.PrefetchScalarGridSpec(
        num_scalar_prefetch=1,
        grid=(seg.nt,),
        in_specs=[
            pl.BlockSpec(memory_space=pl.ANY),
            pl.BlockSpec((seg.tm, ROUTE_LANES), lambda i, ps: (i, 0)),
            pl.BlockSpec((seg.tm, D_MODEL), lambda i, ps: (i, 0)),
            pl.BlockSpec(seg.mod_block(), lambda i, ps: seg.mod_index(i)),
            pl.BlockSpec((1, D_MODEL), row),
            pl.BlockSpec((1, D_MODEL), row),
        ],
        out_specs=pl.BlockSpec((seg.tm, D_MODEL), lambda i, ps: (i, 0)),
        scratch_shapes=[pltpu.VMEM((GATHER_AHEAD + 1, TOP_K, seg.tm, D_MODEL), F32),
                        pltpu.SemaphoreType.DMA((GATHER_AHEAD + 1,))],
    )
    return pl.pallas_call(
        kern,
        out_shape=jax.ShapeDtypeStruct((seg.rows, D_MODEL), F32),
        grid_spec=grid_spec,
        compiler_params=_params("arbitrary"),
        name="moe_combine_postnorm",
    )(pos, y_sorted, route, x, gate, ln_g, ln_b)


def _rope_tables(pos):
    half = HEAD_DIM // 2
    inv = ROPE_THETA ** (-jnp.arange(half, dtype=F32) / half)
    ang = pos.astype(F32)[:, None] * inv[None, :]
    cos = jnp.cos(ang)
    sin = jnp.sin(ang)
    return jnp.concatenate([cos, cos], axis=-1), jnp.concatenate([-sin, sin], axis=-1)


def kernel(x_prompt, x_sample, cache_k, cache_v, state_conv, page_table, c_prompt, c_sample, ada_w, ada_b, ln_g, ln_b, conv_w_in, conv_b_in, conv_dw, conv_dw_b, conv_ln_g, conv_ln_b, conv_w_out, conv_b_out, attn_w_qkv, attn_lambda, attn_subln_g, attn_w_out, moe_w_group, moe_b_group, moe_w_expert, moe_b_expert, moe_w_in, moe_w_out):
    bp, tp, d = x_prompt.shape
    bs, ts, _ = x_sample.shape
    n_pages = page_table.shape[1]
    past_len = n_pages * PAGE_SIZE
    mp, ms = bp * tp, bs * ts
    assert d == D_MODEL and tp % 512 == 0 and ms % 8 == 0

    seg_p = Seg(mp, 512, tp, per_row=False)
    seg_p256 = Seg(mp, 256, tp, per_row=False)
    seg_s = Seg(ms, ms, ts, per_row=True)

    n_c = bp + bs
    c_rows = -(-n_c // 8) * 8
    c_pad = jnp.concatenate([c_prompt, c_sample, jnp.zeros((c_rows - n_c, d), F32)], axis=0)
    mods = ada_all(c_pad, ada_w.reshape(DEPTH * 2, d, 3 * d), ada_b.reshape(DEPTH * 2, 1, 3 * d))

    def mod_params(i, sub):
        m = mods[i * 2 + sub]
        out_p = [m[:bp, c * d:(c + 1) * d].reshape(bp, 1, d) for c in range(3)]
        out_s = [jnp.repeat(m[bp:n_c, c * d:(c + 1) * d], ts, axis=0).reshape(1, ms, d) for c in range(3)]
        return out_p, out_s

    cos_p, sin_p = _rope_tables(jnp.arange(tp, dtype=jnp.int32))
    cos_s, sin_s = _rope_tables(jnp.tile(past_len + jnp.arange(ts, dtype=jnp.int32), bs))

    x_p = x_prompt.reshape(mp, d)
    x_s = x_sample.reshape(ms, d)
    n_attn, pool = cache_k.shape[:2]
    n_hj = 2 * N_HEADS
    cache_k2 = cache_k.reshape(n_attn, pool, PAGE_SIZE * n_hj, HEAD_DIM)
    cache_v2 = cache_v.reshape(n_attn, pool, PAGE_SIZE * N_HEADS, V_DIM)

    n_assign = (mp + ms) * TOP_K
    n_tiles = -(-n_assign // MOE_TM) + N_EXPERTS - 1

    k_p = k_s = None
    v_p_rows, v_s_rows, conv_p_rows, conv_s_rows = [], [], [], []
    row2 = lambda a: a.reshape(1, -1)

    for i in range(DEPTH):
        l = i // N_MIXERS
        (sh_p, sc_p, gt_p), (sh_s, sc_s, gt_s) = mod_params(i, 0)
        lg, lb = row2(ln_g[i, 0]), row2(ln_b[i, 0])
        if i % N_MIXERS == 0:
            w_in = conv_w_in[l].astype(BF16)
            b_in = row2(conv_b_in[l])
            w_out = conv_w_out[l].astype(BF16)
            b_out = row2(conv_b_out[l])
            dw = jnp.concatenate([conv_dw[l], jnp.zeros((HALO - CONV_WIDTH, d), F32)], axis=0)
            cw = (dw, row2(conv_dw_b[l]), row2(conv_ln_g[l]), row2(conv_ln_b[l]))

            glu_p = mod_mm_glu(seg_p, x_p, sh_p, sc_p, w_in, b_in).reshape(bp, tp, d)
            tt = 256
            halo_idx = lambda b, t: (b, jnp.maximum(t * (tt // HALO) - 1, 0), 0)
            y_p = dwconv_ln_silu(glu_p, glu_p, halo_idx, True, tt, *cw).reshape(mp, d)
            conv_p_rows.append(glu_p[:, tp - (CONV_WIDTH - 1):])
            x_p = mm_postnorm(seg_p, y_p, w_out, b_out, x_p, gt_p, lg, lb)

            glu_s = mod_mm_glu(seg_s, x_s, sh_s, sc_s, w_in, b_in).reshape(bs, ts, d)
            hist = state_conv[l].astype(F32)
            halo_s = jnp.concatenate([jnp.zeros((bs, HALO - (CONV_WIDTH - 1), d), F32), hist], axis=1)
            cur_s = jnp.concatenate([glu_s, jnp.zeros((bs, 8 - ts, d), F32)], axis=1)
            y_s = dwconv_ln_silu(cur_s, halo_s, lambda b, t: (b, 0, 0), False, 8, *cw)[:, :ts].reshape(ms, d)
            conv_s_rows.append(jnp.concatenate([hist, glu_s], axis=1)[:, ts:])
            x_s = mm_postnorm(seg_s, y_s, w_out, b_out, x_s, gt_s, lg, lb)
        else:
            lam_init = 0.8 - 0.6 * math.exp(-0.3 * i)
            w_qkv = attn_w_qkv[l].astype(BF16)
            w_out = attn_w_out[l].astype(BF16)
            no_bias = jnp.zeros((1, d), F32)
            lam_vecs = attn_lambda[l].astype(F32)
            sub_g = row2(attn_subln_g[l])

            k_p, v_p, qkv_p_bf = mod_mm_qkv(seg_p, l, n_attn, x_p, sh_p, sc_p, w_qkv, cos_p, sin_p, prev_k=k_p)
            v_p_rows.append(v_p.reshape(bp, tp, N_HEADS, V_DIM))
            o_p = flash_diff_attention(qkv_p_bf, bp, tp, lam_vecs, sub_g, lam_init)
            x_p = mm_postnorm(seg_p, o_p, w_out, no_bias, x_p, gt_p, lg, lb)

            k_s, v_s, qkv_s_bf = mod_mm_qkv(seg_s, l, n_attn, x_s, sh_s, sc_s, w_qkv, cos_s, sin_s, prev_k=k_s)
            v_s_rows.append(v_s.reshape(bs, ts, N_HEADS, V_DIM))
            q_rows = qkv_s_bf[0].astype(F32).reshape(bs, ts, n_hj, HEAD_DIM).transpose(0, 2, 1, 3)
            q_rows = jnp.pad(q_rows, ((0, 0), (0, 0), (0, Q_ROWS - ts), (0, 0))).reshape(bs, n_hj * Q_ROWS, HEAD_DIM)
            k_new = jnp.pad(k_s[l].reshape(bs, ts * n_hj, HEAD_DIM), ((0, 0), (0, (PAGE_SIZE - ts) * n_hj), (0, 0)))
            v_new = jnp.pad(v_s.reshape(bs, ts * N_HEADS, V_DIM), ((0, 0), (0, (PAGE_SIZE - ts) * N_HEADS), (0, 0)))
            o_s = paged_diff_attention(l, page_table, q_rows, cache_k2, cache_v2, k_new, v_new, lam_vecs, sub_g, lam_init)
            o_s = o_s.reshape(bs, N_HEADS, Q_ROWS, V_DIM)[:, :, :ts].transpose(0, 2, 1, 3).reshape(ms, d)
            x_s = mm_postnorm(seg_s, o_s, w_out, no_bias, x_s, gt_s, lg, lb)

        (sh_p, sc_p, gt_p), (sh_s, sc_s, gt_s) = mod_params(i, 1)
        lg, lb = row2(ln_g[i, 1]), row2(ln_b[i, 1])
        n_logit = N_GROUPS + N_EXPERTS
        w_route = jnp.concatenate([moe_w_group[i], moe_w_expert[i], jnp.zeros((d, ROUTE_LANES - n_logit), F32)], axis=1)
        b_route = jnp.concatenate([moe_b_group[i], moe_b_expert[i], jnp.zeros((ROUTE_LANES - n_logit,), F32)]).reshape(1, -1)
        routed = moe_router(seg_p, mp + ms, 0, x_p, sh_p, sc_p, w_route, b_route)
        h_all, route = moe_router(seg_s, mp + ms, mp, x_s, sh_s, sc_s, w_route, b_route, prev=routed)
        ids = route[:, :TOP_K].astype(jnp.int32)
        pos, src_rows, tiles = route_layout(ids, n_tiles)
        y_sorted = moe_experts(i, tiles, src_rows, h_all, moe_w_in, moe_w_out)
        x_p = moe_combine_postnorm(seg_p256, pos[:mp * TOP_K], y_sorted, route[:mp], x_p, gt_p, lg, lb)
        x_s = moe_combine_postnorm(seg_s, pos[mp * TOP_K:], y_sorted, route[mp:], x_s, gt_s, lg, lb)

    return (x_p.reshape(bp, tp, d), x_s.reshape(bs, ts, d),
            k_p.reshape(n_attn, bp, tp, N_HEADS, 2, HEAD_DIM), jnp.stack(v_p_rows), jnp.stack(conv_p_rows),
            k_s.reshape(n_attn, bs, ts, N_HEADS, 2, HEAD_DIM), jnp.stack(v_s_rows), jnp.stack(conv_s_rows))
```
